```python
import math
import jax, jax.numpy as jnp
from jax import lax
import numpy as np


D_MODEL = 4096
BATCH = 2
SEQ = 8192
DEPTH = 2

N_EVEN = (DEPTH + 1) // 2
N_ODD = DEPTH // 2
CHUNK = 64
D_FF = 4 * D_MODEL
MIX_WIDTH = D_MODEL
GROUP_WIDTH = MIX_WIDTH // 2
DEEPNORM_ALPHA = (2.0 * DEPTH) ** 0.25
DEEPNORM_BETA = (8.0 * DEPTH) ** -0.25
EPS = 1e-6

GDN_HEADS = 16
GDN_DK = GROUP_WIDTH // GDN_HEADS
GDN_DV = GROUP_WIDTH // GDN_HEADS
GDN_QK = GDN_HEADS * GDN_DK
GDN_VW = GDN_HEADS * GDN_DV
CONV_K = 4
MLSTM_HEADS = 8
MLSTM_DV = GROUP_WIDTH // MLSTM_HEADS
MLSTM_DK = MLSTM_DV // 2
MLSTM_QK = MLSTM_HEADS * MLSTM_DK
MLSTM_VW = MLSTM_HEADS * MLSTM_DV
RET_HEADS = 8
RET_DV = GROUP_WIDTH // RET_HEADS
RET_DK = RET_DV // 2
RET_QK = RET_HEADS * RET_DK
RET_VW = RET_HEADS * RET_DV
ROPE_BASE = 10000.0
GLA_HEADS = 4
GLA_DV = GROUP_WIDTH // GLA_HEADS
GLA_DK = GLA_DV // 2
GLA_QK = GLA_HEADS * GLA_DK
GLA_VW = GLA_HEADS * GLA_DV
GLA_RANK = 16
GLA_TAU = 16.0

EVEN_SPLITS = (2 * GDN_QK + GDN_VW, GDN_VW, GDN_HEADS, GDN_HEADS, MLSTM_QK, MLSTM_QK, MLSTM_VW, MLSTM_VW, MLSTM_HEADS, MLSTM_HEADS)
ODD_SPLITS = (RET_QK, RET_QK, RET_VW, RET_VW, GLA_QK, GLA_QK, GLA_VW, GLA_VW, GLA_RANK)
EVEN_IN = sum(EVEN_SPLITS)
ODD_IN = sum(ODD_SPLITS)

kernel_name = 'hybrid_gdn_mlstm_retnet_gla_deepnorm'


def _split(y, sizes):
    return jnp.split(y, [int(s) for s in np.cumsum(sizes)[:-1]], axis=-1)


def _heads(t, h):
    b, s, _ = t.shape
    return t.reshape(b, s, h, -1).transpose(0, 2, 1, 3)


def _merge(t):
    b, h, s, d = t.shape
    return t.transpose(0, 2, 1, 3).reshape(b, s, h * d)


def _to_chunks(t):
    b, h, s = t.shape[:3]
    return t.reshape(b, h, s // CHUNK, CHUNK, *t.shape[3:])


def _from_chunks(o):
    n, b, h, l, d = o.shape
    return jnp.moveaxis(o, 0, 2).reshape(b, h, n * l, d)


def _masks():
    causal = jnp.tril(jnp.ones((CHUNK, CHUNK), dtype=bool))
    strict = jnp.tril(jnp.ones((CHUNK, CHUNK), dtype=bool), k=-1)
    return causal, strict


def _rms(t):
    tf = t.astype(jnp.float32)
    return tf * lax.rsqrt(jnp.mean(tf * tf, axis=-1, keepdims=True) + EPS)


def _l2norm(t):
    return t * lax.rsqrt(jnp.sum(t * t, axis=-1, keepdims=True) + EPS)


def _layer_norm(x, g, b):
    xf = x.astype(jnp.float32)
    mu = jnp.mean(xf, axis=-1, keepdims=True)
    var = jnp.mean(jnp.square(xf - mu), axis=-1, keepdims=True)
    return ((xf - mu) * lax.rsqrt(var + 1e-5) * g + b).astype(x.dtype)


def _causal_conv(t, w):
    k, c = w.shape
    return lax.conv_general_dilated(t, w[:, None, :], window_strides=(1,), padding=[(k - 1, 0)],
                                    dimension_numbers=('NWC', 'WIO', 'NWC'), feature_group_count=c)


def _rotate(t, cos, sin):
    half = t.shape[-1] // 2
    t1, t2 = t[..., :half], t[..., half:]
    return jnp.concatenate([t1 * cos - t2 * sin, t1 * sin + t2 * cos], axis=-1)


def gated_delta_rule(q, k, v, g, beta):
    b_, h_, s_, dk = q.shape
    dv = v.shape[-1]
    causal, strict = _masks()
    qc = _to_chunks(q * dk ** -0.5)
    kc = _to_chunks(k)
    vc = _to_chunks(v)
    bc = _to_chunks(beta)
    gam = jnp.cumsum(_to_chunks(g), axis=-1)
    decay = jnp.where(causal, jnp.exp(jnp.where(causal, gam[..., :, None] - gam[..., None, :], 0.0)), 0.0)
    kk = jnp.einsum('bhnid,bhnjd->bhnij', kc, kc)
    a_mat = jnp.where(strict, kk * decay * bc[..., :, None], 0.0)
    eye = jnp.eye(CHUNK, dtype=a_mat.dtype)
    rhs = jnp.concatenate([vc * bc[..., None], kc * (bc * jnp.exp(gam))[..., None]], axis=-1)
    sol = lax.linalg.triangular_solve(a_mat + eye, rhs, left_side=True, lower=True, unit_diagonal=True)
    u, w = sol[..., :dv], sol[..., dv:]
    qk = jnp.einsum('bhnid,bhnjd->bhnij', qc, kc) * decay
    q_dec = qc * jnp.exp(gam)[..., None]
    k_end = kc * jnp.exp(gam[..., -1:] - gam)[..., None]
    c_dec = jnp.exp(gam[..., -1])
    xs = tuple(jnp.moveaxis(t, 2, 0) for t in (u, w, q_dec, qk, k_end, c_dec))

    def step(state, xs_i):
        u_i, w_i, qd_i, qk_i, ke_i, cd_i = xs_i
        v_new = u_i - jnp.einsum('bhld,bhde->bhle', w_i, state)
        o_i = jnp.einsum('bhld,bhde->bhle', qd_i, state) + jnp.einsum('bhij,bhje->bhie', qk_i, v_new)
        state = state * cd_i[..., None, None] + jnp.einsum('bhld,bhle->bhde', ke_i, v_new)
        return state, o_i

    s0 = jnp.zeros((b_, h_, dk, dv), q.dtype)
    _, o = lax.scan(step, s0, xs)
    return _from_chunks(o)


def mlstm_chunkwise(q, k, v, i_pre, f_pre):
    b_, h_, s_, dk = q.shape
    dv = v.shape[-1]
    causal, _ = _masks()
    qc = _to_chunks(q)
    kc = _to_chunks(k * dk ** -0.5)
    vc = _to_chunks(v)
    fc = jnp.cumsum(_to_chunks(jax.nn.log_sigmoid(f_pre)), axis=-1)
    ic = _to_chunks(i_pre)
    d_log = jnp.where(causal, fc[..., :, None] - fc[..., None, :] + ic[..., None, :], -jnp.inf)
    m_intra = jnp.max(d_log, axis=-1)
    qk = jnp.einsum('bhnid,bhnjd->bhnij', qc, kc)
    src_end = fc[..., -1:] - fc + ic
    m_src_end = jnp.max(src_end, axis=-1)
    xs = tuple(jnp.moveaxis(t, 2, 0) for t in (qc, kc, vc, fc, d_log, m_intra, qk, src_end, m_src_end))

    def step(carry, xs_i):
        c_st, n_st, m_st = carry
        q_i, k_i, v_i, f_i, d_i, mi_i, qk_i, se_i, mse_i = xs_i
        m_inter = f_i + m_st[..., None]
        m_t = jnp.maximum(m_inter, mi_i)
        w_inter = jnp.exp(m_inter - m_t)
        w_intra = jnp.exp(d_i - m_t[..., None]) * qk_i
        num = w_inter[..., None] * jnp.einsum('bhld,bhde->bhle', q_i, c_st) + jnp.einsum('bhts,bhse->bhte', w_intra, v_i)
        den = w_inter * jnp.einsum('bhld,bhd->bhl', q_i, n_st) + jnp.sum(w_intra, axis=-1)
        h = num / jnp.maximum(jnp.abs(den), jnp.exp(-m_t))[..., None]
        m_new = jnp.maximum(f_i[..., -1] + m_st, mse_i)
        w_state = jnp.exp(f_i[..., -1] + m_st - m_new)
        w_src = jnp.exp(se_i - m_new[..., None])
        c_st = w_state[..., None, None] * c_st + jnp.einsum('bhld,bhle->bhde', k_i * w_src[..., None], v_i)
        n_st = w_state[..., None] * n_st + jnp.einsum('bhl,bhld->bhd', w_src, k_i)
        return (c_st, n_st, m_new), h

    init = (jnp.zeros((b_, h_, dk, dv), q.dtype), jnp.zeros((b_, h_, dk), q.dtype), jnp.zeros((b_, h_), q.dtype))
    _, hs = lax.scan(step, init, xs)
    return _from_chunks(hs)


def chunk_gated_linear_attention(q, k, v, log_a):
    b_, h_, s_, dk = q.shape
    dv = v.shape[-1]
    causal, _ = _masks()
    qc = _to_chunks(q)
    kc = _to_chunks(k)
    vc = _to_chunks(v)
    bcum = jnp.cumsum(_to_chunks(log_a), axis=-2)
    q_dec = qc * jnp.exp(bcum)
    k_inv = kc * jnp.exp(-bcum)
    attn = jnp.where(causal, jnp.einsum('bhnid,bhnjd->bhnij', q_dec, k_inv), 0.0)
    o_intra = jnp.einsum('bhnij,bhnje->bhnie', attn, vc).reshape(b_, h_, s_, dv)
    k_end = kc * jnp.exp(bcum[..., -1:, :] - bcum)
    end_decay = jnp.exp(bcum[..., -1, :])
    xs = tuple(jnp.moveaxis(t, 2, 0) for t in (q_dec, k_end, vc, end_decay))

    def step(state, xs_i):
        qd_i, ke_i, v_i, ed_i = xs_i
        o_i = jnp.einsum('bhld,bhde->bhle', qd_i, state)
        state = state * ed_i[..., None] + jnp.einsum('bhld,bhle->bhde', ke_i, v_i)
        return state, o_i

    s0 = jnp.zeros((b_, h_, dk, dv), q.dtype)
    _, o_inter = lax.scan(step, s0, xs)
    return o_intra + _from_chunks(o_inter)


def even_mixer(x, w_in, conv_w, a_log, dt_bias, gdn_norm_w, ml_gate_b, ml_norm_w, w_out):
    f32 = jnp.float32
    y = (x @ w_in).astype(f32)
    g_qkv, g_z, g_a, g_b, m_q, m_k, m_v, m_o, m_i, m_f = _split(y, EVEN_SPLITS)
    g_qkv = jax.nn.silu(_causal_conv(g_qkv, conv_w.astype(f32)))
    g_q, g_k, g_v = _split(g_qkv, (GDN_QK, GDN_QK, GDN_VW))
    q = _l2norm(_heads(g_q, GDN_HEADS))
    k = _l2norm(_heads(g_k, GDN_HEADS))
    v = _heads(g_v, GDN_HEADS)
    beta = jax.nn.sigmoid(g_b).transpose(0, 2, 1)
    g = (-jnp.exp(a_log.astype(f32)) * jax.nn.softplus(g_a + dt_bias)).transpose(0, 2, 1)
    o_a = gated_delta_rule(q, k, v, g, beta)
    o_a = _rms(o_a) * gdn_norm_w * jax.nn.silu(_heads(g_z, GDN_HEADS))
    i_pre = (m_i + ml_gate_b[:MLSTM_HEADS]).transpose(0, 2, 1)
    f_pre = (m_f + ml_gate_b[MLSTM_HEADS:]).transpose(0, 2, 1)
    o_b = mlstm_chunkwise(_heads(m_q, MLSTM_HEADS), _heads(m_k, MLSTM_HEADS), _heads(m_v, MLSTM_HEADS), i_pre, f_pre)
    o_b = _rms(o_b) * ml_norm_w * jax.nn.sigmoid(_heads(m_o, MLSTM_HEADS))
    mixed = jnp.concatenate([_merge(o_a), _merge(o_b)], axis=-1).astype(x.dtype)
    return mixed @ w_out


def odd_mixer(x, positions, w_in, gla_w_up, gla_b_up, gla_norm_w, w_out):
    f32 = jnp.float32
    b_, s_, _ = x.shape
    y = (x @ w_in).astype(f32)
    r_q, r_k, r_v, r_g, l_q, l_k, l_v, l_r, l_a = _split(y, ODD_SPLITS)
    inv_freq = 1.0 / (ROPE_BASE ** jnp.linspace(0.0, 1.0, RET_DK // 2, dtype=f32))
    theta = positions.astype(f32)[:, None, :, None] * inv_freq
    cos, sin = jnp.cos(theta), jnp.sin(theta)
    q = _rotate(_heads(r_q, RET_HEADS), cos, sin)
    k = _rotate(_heads(r_k, RET_HEADS), cos, sin) * RET_DK ** -0.5
    v = _heads(r_v, RET_HEADS)
    log_gamma = jnp.log(1.0 - 2.0 ** (-5.0 - jnp.arange(RET_HEADS, dtype=f32)))
    log_a = jnp.broadcast_to(log_gamma[None, :, None, None], (b_, RET_HEADS, s_, 1))
    o_c = chunk_gated_linear_attention(q, k, v, log_a)
    o_c = _rms(o_c) * jax.nn.silu(_heads(r_g, RET_HEADS))
    gate = l_a @ gla_w_up.astype(f32) + gla_b_up
    log_alpha = _heads(jax.nn.log_sigmoid(gate) / GLA_TAU, GLA_HEADS)
    o_d = chunk_gated_linear_attention(_heads(l_q, GLA_HEADS) * GLA_DK ** -0.5, _heads(l_k, GLA_HEADS), _heads(l_v, GLA_HEADS), log_alpha)
    o_d = _rms(o_d) * gla_norm_w * jax.nn.silu(_heads(l_r, GLA_HEADS))
    mixed = jnp.concatenate([_merge(o_c), _merge(o_d)], axis=-1).astype(x.dtype)
    return mixed @ w_out


def _sqrelu_mlp(x, w_up, w_down):
    return jnp.square(jax.nn.relu(x @ w_up)) @ w_down


def setup_inputs(seed: int = 0) -> dict:
    key = jax.random.key(seed)
    ks = jax.random.split(key, 24)
    f32 = jnp.float32

    def nrm(k, shape, scale):
        return jax.random.normal(k, shape, f32) * scale

    x = nrm(ks[0], (BATCH, SEQ, D_MODEL), 1.0)
    start = jax.random.randint(ks[1], (BATCH, 1), 0, 1024, dtype=jnp.int32)
    positions = start + jnp.arange(SEQ, dtype=jnp.int32)[None, :]
    e_w_in = nrm(ks[2], (N_EVEN, D_MODEL, EVEN_IN), D_MODEL ** -0.5)
    e_conv_w = nrm(ks[3], (N_EVEN, CONV_K, 2 * GDN_QK + GDN_VW), CONV_K ** -0.5)
    e_a_log = jnp.log(jax.random.uniform(ks[4], (N_EVEN, GDN_HEADS), f32, 1.0, 16.0))
    dt = jnp.exp(jax.random.uniform(ks[5], (N_EVEN, GDN_HEADS), f32, math.log(1e-3), math.log(1e-1)))
    e_dt_bias = dt + jnp.log(-jnp.expm1(-dt))
    e_gdn_norm_w = 1.0 + nrm(ks[6], (N_EVEN, GDN_DV), 0.02)
    e_mlstm_gate_b = jnp.concatenate([nrm(ks[7], (N_EVEN, MLSTM_HEADS), 0.1),
                                      jnp.linspace(3.0, 6.0, MLSTM_HEADS, dtype=f32)[None, :] + nrm(ks[8], (N_EVEN, MLSTM_HEADS), 0.1)], axis=-1)
    e_mlstm_norm_w = 1.0 + nrm(ks[9], (N_EVEN, MLSTM_DV), 0.02)
    e_w_out = nrm(ks[10], (N_EVEN, MIX_WIDTH, D_MODEL), MIX_WIDTH ** -0.5 * DEEPNORM_BETA)
    o_w_in = nrm(ks[11], (N_ODD, D_MODEL, ODD_IN), D_MODEL ** -0.5)
    o_gla_w_up = nrm(ks[12], (N_ODD, GLA_RANK, GLA_QK), GLA_RANK ** -0.5)
    o_gla_b_up = nrm(ks[13], (N_ODD, GLA_QK), 0.02)
    o_gla_norm_w = 1.0 + nrm(ks[14], (N_ODD, GLA_DV), 0.02)
    o_w_out = nrm(ks[15], (N_ODD, MIX_WIDTH, D_MODEL), MIX_WIDTH ** -0.5 * DEEPNORM_BETA)
    ln_mix_g = 1.0 + nrm(ks[16], (DEPTH, D_MODEL), 0.02)
    ln_mix_b = nrm(ks[17], (DEPTH, D_MODEL), 0.02)
    mlp_w_up = nrm(ks[18], (DEPTH, D_MODEL, D_FF), D_MODEL ** -0.5)
    mlp_w_down = nrm(ks[19], (DEPTH, D_FF, D_MODEL), D_FF ** -0.5 * DEEPNORM_BETA)
    ln_mlp_g = 1.0 + nrm(ks[20], (DEPTH, D_MODEL), 0.02)
    ln_mlp_b = nrm(ks[21], (DEPTH, D_MODEL), 0.02)
    return {'x': x, 'positions': positions,
            'e_w_in': e_w_in, 'e_conv_w': e_conv_w, 'e_a_log': e_a_log, 'e_dt_bias': e_dt_bias,
            'e_gdn_norm_w': e_gdn_norm_w, 'e_mlstm_gate_b': e_mlstm_gate_b, 'e_mlstm_norm_w': e_mlstm_norm_w,
            'e_w_out': e_w_out,
            'o_w_in': o_w_in, 'o_gla_w_up': o_gla_w_up, 'o_gla_b_up': o_gla_b_up, 'o_gla_norm_w': o_gla_norm_w,
            'o_w_out': o_w_out,
            'ln_mix_g': ln_mix_g, 'ln_mix_b': ln_mix_b, 'mlp_w_up': mlp_w_up, 'mlp_w_down': mlp_w_down,
            'ln_mlp_g': ln_mlp_g, 'ln_mlp_b': ln_mlp_b}


def reference(x, positions, e_w_in, e_conv_w, e_a_log, e_dt_bias, e_gdn_norm_w, e_mlstm_gate_b, e_mlstm_norm_w,
              e_w_out, o_w_in, o_gla_w_up, o_gla_b_up, o_gla_norm_w, o_w_out,
              ln_mix_g, ln_mix_b, mlp_w_up, mlp_w_down, ln_mlp_g, ln_mlp_b):
    for layer in range(DEPTH):
        j = layer // 2
        if layer % 2 == 0:
            h = even_mixer(x, e_w_in[j], e_conv_w[j], e_a_log[j], e_dt_bias[j], e_gdn_norm_w[j],
                           e_mlstm_gate_b[j], e_mlstm_norm_w[j], e_w_out[j])
        else:
            h = odd_mixer(x, positions, o_w_in[j], o_gla_w_up[j], o_gla_b_up[j], o_gla_norm_w[j], o_w_out[j])
        x = _layer_norm(DEEPNORM_ALPHA * x + h.astype(x.dtype), ln_mix_g[layer], ln_mix_b[layer])
        f = _sqrelu_mlp(x, mlp_w_up[layer], mlp_w_down[layer])
        x = _layer_norm(DEEPNORM_ALPHA * x + f.astype(x.dtype), ln_mlp_g[layer], ln_mlp_b[layer])
    return x
```

```python
import functools
import math

import jax
import jax.numpy as jnp
from jax import lax
from jax.experimental import pallas as pl
from jax.experimental.pallas import tpu as pltpu

F32 = jnp.float32
BF16 = jnp.bfloat16
HIGHEST = lax.Precision.HIGHEST

D_MODEL = 4096
DEPTH = 2
CHUNK = 64
D_FF = 4 * D_MODEL
GROUP_WIDTH = D_MODEL // 2
ALPHA = (2.0 * DEPTH) ** 0.25
EPS = 1e-6
LN_EPS = 1e-5

GDN_HEADS = 16
GDN_D = GROUP_WIDTH // GDN_HEADS
CONV_K = 4
GDN_QKV = 3 * GROUP_WIDTH
ML_HEADS = 8
ML_DV = GROUP_WIDTH // ML_HEADS
ML_DK = ML_DV // 2
RET_HEADS = 8
RET_DV = GROUP_WIDTH // RET_HEADS
RET_DK = RET_DV // 2
ROPE_BASE = 10000.0
GLA_HEADS = 4
GLA_DV = GROUP_WIDTH // GLA_HEADS
GLA_DK = GLA_DV // 2
GLA_RANK = 16
GLA_TAU = 16.0

LANES = 128
EVEN_MAIN = GDN_QKV + GROUP_WIDTH + 2 * ML_HEADS * ML_DK + 2 * GROUP_WIDTH
ODD_MAIN = 2 * RET_HEADS * RET_DK + 2 * GROUP_WIDTH + 2 * GLA_HEADS * GLA_DK + 2 * GROUP_WIDTH
GA_OFF, GB_OFF, MI_OFF, MF_OFF = 0, GDN_HEADS, 2 * GDN_HEADS, 2 * GDN_HEADS + ML_HEADS

VMEM_LIMIT = 56 * 1024 * 1024


def _sigmoid(x):
    return 1.0 / (1.0 + jnp.exp(-x))


def _softplus(x):
    return jnp.maximum(x, 0.0) + jnp.log1p(jnp.exp(-jnp.abs(x)))


def _log_sigmoid(x):
    return -_softplus(-x)


def _dot(a, b):
    return jnp.dot(a, b, preferred_element_type=F32)


def _dot_nt(a, b):
    return lax.dot_general(a, b, (((1,), (1,)), ((), ())), preferred_element_type=F32)


def _dot_tn(a, b):
    return _dot(a.T, b)


def _chunk_masks():
    row = lax.broadcasted_iota(jnp.int32, (CHUNK, CHUNK), 0)
    col = lax.broadcasted_iota(jnp.int32, (CHUNK, CHUNK), 1)
    return row, col


def _cumsum_rows(x, row, col):
    tril = jnp.where(row >= col, 1.0, 0.0).astype(F32)
    return jnp.dot(tril, x, precision=HIGHEST, preferred_element_type=F32)


def _unit_lower_inverse(a, row, col):
    eye = jnp.where(row == col, 1.0, 0.0).astype(F32)
    same = (row >> 1) == (col >> 1)
    inv = eye - jnp.where(same, a, 0.0)
    shift = 2
    while (1 << (shift - 1)) < CHUNK:
        same2 = (row >> shift) == (col >> shift)
        off = jnp.where(jnp.logical_and(same2, jnp.logical_not(same)), a, 0.0)
        inv = inv - _dot(inv, _dot(off, inv))
        same = same2
        shift += 1
    return inv


def _mm_kernel_single(a_ref, b_ref, o_ref, *, relu2):
    r = _dot(a_ref[...], b_ref[...])
    if relu2:
        r = jnp.square(jnp.maximum(r, 0.0))
    o_ref[...] = r.astype(o_ref.dtype)


def _mm_kernel_acc(a_ref, b_ref, o_ref, acc_ref, *, nk, relu2):
    k = pl.program_id(2)

    @pl.when(k == 0)
    def _():
        acc_ref[...] = jnp.zeros_like(acc_ref)

    acc_ref[...] += _dot(a_ref[...], b_ref[...])

    @pl.when(k == nk - 1)
    def _():
        r = acc_ref[...]
        if relu2:
            r = jnp.square(jnp.maximum(r, 0.0))
        o_ref[...] = r.astype(o_ref.dtype)


def _matmul(a, b, *, out_dtype, tm, tn, tk, relu2=False, name):
    m, k = a.shape
    _, n = b.shape
    tm, tn, tk = min(tm, m), min(tn, n), min(tk, k)
    assert m % tm == 0 and n % tn == 0 and k % tk == 0
    nk = k // tk
    if nk == 1:
        return pl.pallas_call(
            functools.partial(_mm_kernel_single, relu2=relu2),
            grid=(m // tm, n // tn),
            in_specs=[pl.BlockSpec((tm, k), lambda i, j: (i, 0)),
                      pl.BlockSpec((k, tn), lambda i, j: (0, j))],
            out_specs=pl.BlockSpec((tm, tn), lambda i, j: (i, j)),
            out_shape=jax.ShapeDtypeStruct((m, n), out_dtype),
            compiler_params=pltpu.CompilerParams(
                dimension_semantics=("parallel", "parallel"), vmem_limit_bytes=VMEM_LIMIT),
            name=name,
        )(a, b)
    return pl.pallas_call(
        functools.partial(_mm_kernel_acc, nk=nk, relu2=relu2),
        grid=(m // tm, n // tn, nk),
        in_specs=[pl.BlockSpec((tm, tk), lambda i, j, kk: (i, kk)),
                  pl.BlockSpec((tk, tn), lambda i, j, kk: (kk, j))],
        out_specs=pl.BlockSpec((tm, tn), lambda i, j, kk: (i, j)),
        out_shape=jax.ShapeDtypeStruct((m, n), out_dtype),
        scratch_shapes=[pltpu.VMEM((tm, tn), F32)],
        compiler_params=pltpu.CompilerParams(
            dimension_semantics=("parallel", "parallel", "arbitrary"), vmem_limit_bytes=VMEM_LIMIT),
        name=name,
    )(a, b)


def _ln_kernel(x_ref, h_ref, g_ref, b_ref, o_ref, ob_ref):
    t = ALPHA * x_ref[...] + h_ref[...]
    mu = jnp.mean(t, axis=-1, keepdims=True)
    d = t - mu
    var = jnp.mean(d * d, axis=-1, keepdims=True)
    r = d * lax.rsqrt(var + LN_EPS) * g_ref[...] + b_ref[...]
    o_ref[...] = r
    ob_ref[...] = r.astype(BF16)


def _ln_residual(x, h, g, b, *, tm=256, name):
    m, d = x.shape
    tm = min(tm, m)
    row = pl.BlockSpec((tm, d), lambda i: (i, 0))
    vec = pl.BlockSpec((1, d), lambda i: (0, 0))
    return pl.pallas_call(
        _ln_kernel,
        grid=(m // tm,),
        in_specs=[row, row, vec, vec],
        out_specs=[row, row],
        out_shape=[jax.ShapeDtypeStruct((m, d), F32), jax.ShapeDtypeStruct((m, d), BF16)],
        compiler_params=pltpu.CompilerParams(
            dimension_semantics=("parallel",), vmem_limit_bytes=VMEM_LIMIT),
        name=name,
    )(x, h, g.reshape(1, d), b.reshape(1, d))


def _gdn_kernel(qkv_ref, z_ref, gate_ref, convw_ref, alog_ref, dtb_ref, normw_ref, o_ref, cbuf, s_ref):
    c = pl.program_id(1)
    hist = 8

    @pl.when(c == 0)
    def _():
        cbuf[0:hist, :] = jnp.zeros((hist, GDN_QKV), F32)
        s_ref[...] = jnp.zeros_like(s_ref)

    cbuf[hist:hist + CHUNK, :] = qkv_ref[0]
    row, col = _chunk_masks()
    causal = row >= col
    strict = row > col

    gates = gate_ref[0]
    g_all = -jnp.exp(alog_ref[...]) * _softplus(gates + dtb_ref[...])
    beta_all = _sigmoid(gates)
    gam_all = _cumsum_rows(g_all, row, col)
    gam_t = gam_all.T

    def conv_silu(off):
        acc = cbuf[hist:hist + CHUNK, off:off + GDN_D] * convw_ref[CONV_K - 1:CONV_K, off:off + GDN_D]
        for s in range(1, CONV_K):
            acc = acc + (cbuf[hist - s:hist - s + CHUNK, off:off + GDN_D]
                         * convw_ref[CONV_K - 1 - s:CONV_K - s, off:off + GDN_D])
        return acc * _sigmoid(acc)

    for h in range(GDN_HEADS):
        q = conv_silu(h * GDN_D)
        k = conv_silu(GROUP_WIDTH + h * GDN_D)
        v = conv_silu(2 * GROUP_WIDTH + h * GDN_D)
        q = q * (lax.rsqrt(jnp.sum(q * q, axis=-1, keepdims=True) + EPS) * GDN_D ** -0.5)
        k = k * lax.rsqrt(jnp.sum(k * k, axis=-1, keepdims=True) + EPS)
        gam_c = gam_all[:, GA_OFF + h:GA_OFF + h + 1]
        gam_r = gam_t[GA_OFF + h:GA_OFF + h + 1, :]
        beta_c = beta_all[:, GB_OFF + h:GB_OFF + h + 1]
        decay = jnp.where(causal, jnp.exp(jnp.where(causal, gam_c - gam_r, 0.0)), 0.0)
        a_mat = jnp.where(strict, _dot_nt(k, k) * decay * beta_c, 0.0)
        inv = _unit_lower_inverse(a_mat, row, col)
        egam = jnp.exp(gam_c)
        u = _dot(inv, v * beta_c)
        w = _dot(inv, k * (beta_c * egam))
        qk = _dot_nt(q, k) * decay
        state = s_ref[h]
        v_new = u - _dot(w, state)
        o = _dot(q * egam, state) + _dot(qk, v_new)
        gam_last = gam_c[CHUNK - 1:CHUNK, :]
        k_end = k * jnp.exp(gam_last - gam_c)
        s_ref[h] = state * jnp.exp(gam_last) + _dot_tn(k_end, v_new)
        zz = z_ref[0, :, h * GDN_D:(h + 1) * GDN_D]
        o = o * lax.rsqrt(jnp.mean(o * o, axis=-1, keepdims=True) + EPS) * normw_ref[...]
        o_ref[0, :, h * GDN_D:(h + 1) * GDN_D] = (o * (zz * _sigmoid(zz))).astype(o_ref.dtype)

    cbuf[0:hist, :] = cbuf[CHUNK:CHUNK + hist, :]


def _gdn_mixer(y, yg, conv_w, alog_row, dtb_row, norm_w):
    b, t, _ = y.shape
    n = t // CHUNK
    return pl.pallas_call(
        _gdn_kernel,
        grid=(b, n),
        in_specs=[
            pl.BlockSpec((1, CHUNK, GDN_QKV), lambda i, c: (i, c, 0)),
            pl.BlockSpec((1, CHUNK, GROUP_WIDTH), lambda i, c: (i, c, GDN_QKV // GROUP_WIDTH)),
            pl.BlockSpec((1, CHUNK, LANES), lambda i, c: (i, c, 0)),
            pl.BlockSpec((CONV_K, GDN_QKV), lambda i, c: (0, 0)),
            pl.BlockSpec((1, LANES), lambda i, c: (0, 0)),
            pl.BlockSpec((1, LANES), lambda i, c: (0, 0)),
            pl.BlockSpec((1, GDN_D), lambda i, c: (0, 0)),
        ],
        out_specs=pl.BlockSpec((1, CHUNK, GROUP_WIDTH), lambda i, c: (i, c, 0)),
        out_shape=jax.ShapeDtypeStruct((b, t, GROUP_WIDTH), BF16),
        scratch_shapes=[pltpu.VMEM((CHUNK + 8, GDN_QKV), F32),
                        pltpu.VMEM((GDN_HEADS, GDN_D, GDN_D), F32)],
        compiler_params=pltpu.CompilerParams(
            dimension_semantics=("arbitrary", "arbitrary"), vmem_limit_bytes=VMEM_LIMIT),
        name="gdn_mixer",
    )(y, y, yg, conv_w, alog_row, dtb_row, norm_w)


def _mlstm_kernel(q_ref, k_ref, v_ref, og_ref, gate_ref, gb_ref, normw_ref, o_ref, c_ref, n_ref, m_ref):
    c = pl.program_id(1)

    @pl.when(c == 0)
    def _():
        c_ref[...] = jnp.zeros_like(c_ref)
        n_ref[...] = jnp.zeros_like(n_ref)
        m_ref[...] = jnp.zeros_like(m_ref)

    row, col = _chunk_masks()
    causal = row >= col
    pre = gate_ref[0] + gb_ref[...]
    fc_all = _cumsum_rows(_log_sigmoid(pre), row, col)
    fc_t = fc_all.T
    pre_t = pre.T

    for h in range(ML_HEADS):
        q = q_ref[0, :, h * ML_DK:(h + 1) * ML_DK]
        k = k_ref[0, :, h * ML_DK:(h + 1) * ML_DK] * ML_DK ** -0.5
        v = v_ref[0, :, h * ML_DV:(h + 1) * ML_DV]
        fc_c = fc_all[:, MF_OFF + h:MF_OFF + h + 1]
        fc_r = fc_t[MF_OFF + h:MF_OFF + h + 1, :]
        ic_c = pre[:, MI_OFF + h:MI_OFF + h + 1]
        ic_r = pre_t[MI_OFF + h:MI_OFF + h + 1, :]
        fc_last = fc_c[CHUNK - 1:CHUNK, :]
        d_log = jnp.where(causal, fc_c - fc_r + ic_r, -jnp.inf)
        m_intra = jnp.max(d_log, axis=-1, keepdims=True)
        qk = _dot_nt(q, k)
        m_src_end = jnp.max(fc_last - fc_r + ic_r, axis=-1, keepdims=True)
        m_st = m_ref[h, 0:1, 0:1]
        c_st = c_ref[h]
        n_st = n_ref[h, 0:1, :]
        m_inter = fc_c + m_st
        m_t = jnp.maximum(m_inter, m_intra)
        w_inter = jnp.exp(m_inter - m_t)
        w_intra = jnp.exp(d_log - m_t) * qk
        num = w_inter * _dot(q, c_st) + _dot(w_intra, v)
        den = w_inter * jnp.sum(q * n_st, axis=-1, keepdims=True) + jnp.sum(w_intra, axis=-1, keepdims=True)
        hh = num / jnp.maximum(jnp.abs(den), jnp.exp(-m_t))
        m_new = jnp.maximum(fc_last + m_st, m_src_end)
        w_state = jnp.exp(fc_last + m_st - m_new)
        kw = k * jnp.exp(fc_last - fc_c + ic_c - m_new)
        c_ref[h] = w_state * c_st + _dot_tn(kw, v)
        n_ref[h] = jnp.broadcast_to(w_state * n_st + jnp.sum(kw, axis=0, keepdims=True), (8, ML_DK))
        m_ref[h] = jnp.broadcast_to(m_new, (8, LANES))
        gg = og_ref[0, :, h * ML_DV:(h + 1) * ML_DV]
        hh = hh * lax.rsqrt(jnp.mean(hh * hh, axis=-1, keepdims=True) + EPS) * normw_ref[...]
        o_ref[0, :, h * ML_DV:(h + 1) * ML_DV] = (hh * _sigmoid(gg)).astype(o_ref.dtype)


def _mlstm_mixer(y, yg, gb_row, norm_w):
    b, t, _ = y.shape
    n = t // CHUNK
    qk_w = ML_HEADS * ML_DK
    base = GDN_QKV + GROUP_WIDTH
    return pl.pallas_call(
        _mlstm_kernel,
        grid=(b, n),
        in_specs=[
            pl.BlockSpec((1, CHUNK, qk_w), lambda i, c: (i, c, base // qk_w)),
            pl.BlockSpec((1, CHUNK, qk_w), lambda i, c: (i, c, base // qk_w + 1)),
            pl.BlockSpec((1, CHUNK, GROUP_WIDTH), lambda i, c: (i, c, (base + 2 * qk_w) // GROUP_WIDTH)),
            pl.BlockSpec((1, CHUNK, GROUP_WIDTH), lambda i, c: (i, c, (base + 2 * qk_w) // GROUP_WIDTH + 1)),
            pl.BlockSpec((1, CHUNK, LANES), lambda i, c: (i, c, 0)),
            pl.BlockSpec((1, LANES), lambda i, c: (0, 0)),
            pl.BlockSpec((1, ML_DV), lambda i, c: (0, 0)),
        ],
        out_specs=pl.BlockSpec((1, CHUNK, GROUP_WIDTH), lambda i, c: (i, c, 0)),
        out_shape=jax.ShapeDtypeStruct((b, t, GROUP_WIDTH), BF16),
        scratch_shapes=[pltpu.VMEM((ML_HEADS, ML_DK, ML_DV), F32),
                        pltpu.VMEM((ML_HEADS, 8, ML_DK), F32),
                        pltpu.VMEM((ML_HEADS, 8, LANES), F32)],
        compiler_params=pltpu.CompilerParams(
            dimension_semantics=("arbitrary", "arbitrary"), vmem_limit_bytes=VMEM_LIMIT),
        name="mlstm_mixer",
    )(y, y, y, y, yg, gb_row, norm_w)


def _gla_chunk(q, k, v, bcum, bcum_last, end_decay_col, state, causal):
    q_dec = q * jnp.exp(bcum)
    k_inv = k * jnp.exp(-bcum)
    attn = jnp.where(causal, _dot_nt(q_dec, k_inv), 0.0)
    o = _dot(attn, v) + _dot(q_dec, state)
    k_end = k * jnp.exp(bcum_last - bcum)
    new_state = state * end_decay_col + _dot_tn(k_end, v)
    return o, new_state


def _ret_kernel(q_ref, k_ref, v_ref, g_ref, cos_ref, sin_ref, o_ref, s_ref):
    c = pl.program_id(1)

    @pl.when(c == 0)
    def _():
        s_ref[...] = jnp.zeros_like(s_ref)

    row, col = _chunk_masks()
    causal = row >= col
    cos2 = cos_ref[0]
    sin2 = sin_ref[0]
    steps = (lax.broadcasted_iota(jnp.int32, (CHUNK, 1), 0) + 1).astype(F32)

    for h in range(RET_HEADS):
        log_gamma = math.log(1.0 - 2.0 ** (-5.0 - h))
        q = q_ref[0, :, h * RET_DK:(h + 1) * RET_DK]
        k = k_ref[0, :, h * RET_DK:(h + 1) * RET_DK]
        v = v_ref[0, :, h * RET_DV:(h + 1) * RET_DV]
        q = q * cos2 + pltpu.roll(q, RET_DK // 2, 1) * sin2
        k = (k * cos2 + pltpu.roll(k, RET_DK // 2, 1) * sin2) * RET_DK ** -0.5
        bcum = steps * log_gamma
        bcum_last = bcum[CHUNK - 1:CHUNK, :]
        o, s_ref[h] = _gla_chunk(q, k, v, bcum, bcum_last, jnp.exp(bcum_last), s_ref[h], causal)
        gg = g_ref[0, :, h * RET_DV:(h + 1) * RET_DV]
        o = o * lax.rsqrt(jnp.mean(o * o, axis=-1, keepdims=True) + EPS)
        o_ref[0, :, h * RET_DV:(h + 1) * RET_DV] = (o * (gg * _sigmoid(gg))).astype(o_ref.dtype)


def _ret_mixer(y, cos2, sin2):
    b, t, _ = y.shape
    n = t // CHUNK
    qk_w = RET_HEADS * RET_DK
    return pl.pallas_call(
        _ret_kernel,
        grid=(b, n),
        in_specs=[
            pl.BlockSpec((1, CHUNK, qk_w), lambda i, c: (i, c, 0)),
            pl.BlockSpec((1, CHUNK, qk_w), lambda i, c: (i, c, 1)),
            pl.BlockSpec((1, CHUNK, GROUP_WIDTH), lambda i, c: (i, c, 1)),
            pl.BlockSpec((1, CHUNK, GROUP_WIDTH), lambda i, c: (i, c, 2)),
            pl.BlockSpec((1, CHUNK, RET_DK), lambda i, c: (i, c, 0)),
            pl.BlockSpec((1, CHUNK, RET_DK), lambda i, c: (i, c, 0)),
        ],
        out_specs=pl.BlockSpec((1, CHUNK, GROUP_WIDTH), lambda i, c: (i, c, 0)),
        out_shape=jax.ShapeDtypeStruct((b, t, GROUP_WIDTH), BF16),
        scratch_shapes=[pltpu.VMEM((RET_HEADS, RET_DK, RET_DV), F32)],
        compiler_params=pltpu.CompilerParams(
            dimension_semantics=("arbitrary", "arbitrary"), vmem_limit_bytes=VMEM_LIMIT),
        name="ret_mixer",
    )(y, y, y, y, cos2, sin2)


def _gla_kernel(q_ref, k_ref, v_ref, r_ref, la_ref, wup_ref, bup_ref, normw_ref, o_ref, s_ref):
    c = pl.program_id(1)

    @pl.when(c == 0)
    def _():
        s_ref[...] = jnp.zeros_like(s_ref)

    row, col = _chunk_masks()
    causal = row >= col
    gate = _dot(la_ref[0], wup_ref[...]) + bup_ref[...]
    bcum_all = _cumsum_rows(_log_sigmoid(gate) * (1.0 / GLA_TAU), row, col)

    for h in range(GLA_HEADS):
        q = q_ref[0, :, h * GLA_DK:(h + 1) * GLA_DK] * GLA_DK ** -0.5
        k = k_ref[0, :, h * GLA_DK:(h + 1) * GLA_DK]
        v = v_ref[0, :, h * GLA_DV:(h + 1) * GLA_DV]
        bcum = bcum_all[:, h * GLA_DK:(h + 1) * GLA_DK]
        bcum_last = bcum[CHUNK - 1:CHUNK, :]
        end_decay_col = jnp.exp(bcum.T[:, CHUNK - 1:CHUNK])
        o, s_ref[h] = _gla_chunk(q, k, v, bcum, bcum_last, end_decay_col, s_ref[h], causal)
        rr = r_ref[0, :, h * GLA_DV:(h + 1) * GLA_DV]
        o = o * lax.rsqrt(jnp.mean(o * o, axis=-1, keepdims=True) + EPS) * normw_ref[...]
        o_ref[0, :, h * GLA_DV:(h + 1) * GLA_DV] = (o * (rr * _sigmoid(rr))).astype(o_ref.dtype)


def _gla_mixer(y, yla, w_up_pad, b_up, norm_w):
    b, t, _ = y.shape
    n = t // CHUNK
    qk_w = GLA_HEADS * GLA_DK
    base = 2 * RET_HEADS * RET_DK + 2 * GROUP_WIDTH
    return pl.pallas_call(
        _gla_kernel,
        grid=(b, n),
        in_specs=[
            pl.BlockSpec((1, CHUNK, qk_w), lambda i, c: (i, c, base // qk_w)),
            pl.BlockSpec((1, CHUNK, qk_w), lambda i, c: (i, c, base // qk_w + 1)),
            pl.BlockSpec((1, CHUNK, GROUP_WIDTH), lambda i, c: (i, c, (base + 2 * qk_w) // GROUP_WIDTH)),
            pl.BlockSpec((1, CHUNK, GROUP_WIDTH), lambda i, c: (i, c, (base + 2 * qk_w) // GROUP_WIDTH + 1)),
            pl.BlockSpec((1, CHUNK, LANES), lambda i, c: (i, c, 0)),
            pl.BlockSpec((LANES, qk_w), lambda i, c: (0, 0)),
            pl.BlockSpec((1, qk_w), lambda i, c: (0, 0)),
            pl.BlockSpec((1, GLA_DV), lambda i, c: (0, 0)),
        ],
        out_specs=pl.BlockSpec((1, CHUNK, GROUP_WIDTH), lambda i, c: (i, c, 0)),
        out_shape=jax.ShapeDtypeStruct((b, t, GROUP_WIDTH), BF16),
        scratch_shapes=[pltpu.VMEM((GLA_HEADS, GLA_DK, GLA_DV), F32)],
        compiler_params=pltpu.CompilerParams(
            dimension_semantics=("arbitrary", "arbitrary"), vmem_limit_bytes=VMEM_LIMIT),
        name="gla_mixer",
    )(y, y, y, y, yla, w_up_pad, b_up, norm_w)


def _pad_lanes(v, offset=0):
    return jnp.zeros((1, LANES), F32).at[0, offset:offset + v.shape[0]].set(v.astype(F32))


def _even_mixer(xb, w_in, conv_w, a_log, dt_bias, gdn_norm_w, ml_gate_b, ml_norm_w, w_out):
    b, t, d = xb.shape
    x2 = xb.reshape(b * t, d)
    g_end = GDN_QKV + GROUP_WIDTH
    m_start = g_end + 2 * GDN_HEADS
    m_end = m_start + EVEN_MAIN - g_end
    w_main = jnp.concatenate([w_in[:, :g_end], w_in[:, m_start:m_end]], axis=1).astype(BF16)
    w_gate = jnp.concatenate([w_in[:, g_end:m_start], w_in[:, m_end:]], axis=1)
    w_gate = jnp.pad(w_gate, ((0, 0), (0, LANES - w_gate.shape[1]))).astype(BF16)
    y = _matmul(x2, w_main, out_dtype=F32, tm=1024, tn=1024, tk=d, name="even_in_proj").reshape(b, t, EVEN_MAIN)
    yg = _matmul(x2, w_gate, out_dtype=F32, tm=1024, tn=LANES, tk=d, name="even_gate_proj").reshape(b, t, LANES)
    o_a = _gdn_mixer(y, yg, conv_w.astype(F32), _pad_lanes(a_log, GA_OFF), _pad_lanes(dt_bias, GA_OFF),
                     gdn_norm_w.reshape(1, GDN_D).astype(F32))
    o_b = _mlstm_mixer(y, yg, _pad_lanes(ml_gate_b, MI_OFF), ml_norm_w.reshape(1, ML_DV).astype(F32))
    mixed = jnp.concatenate([o_a, o_b], axis=-1).reshape(b * t, d)
    return _matmul(mixed, w_out.astype(BF16), out_dtype=F32, tm=1024, tn=1024, tk=d, name="even_out_proj")


def _odd_mixer(xb, positions, w_in, gla_w_up, gla_b_up, gla_norm_w, w_out):
    b, t, d = xb.shape
    x2 = xb.reshape(b * t, d)
    w_main = w_in[:, :ODD_MAIN].astype(BF16)
    w_la = jnp.pad(w_in[:, ODD_MAIN:], ((0, 0), (0, LANES - GLA_RANK))).astype(BF16)
    y = _matmul(x2, w_main, out_dtype=F32, tm=1024, tn=1024, tk=d, name="odd_in_proj").reshape(b, t, ODD_MAIN)
    yla = _matmul(x2, w_la, out_dtype=F32, tm=1024, tn=LANES, tk=d, name="odd_gate_proj").reshape(b, t, LANES)
    inv_freq = 1.0 / (ROPE_BASE ** jnp.linspace(0.0, 1.0, RET_DK // 2, dtype=F32))
    theta = positions.astype(F32)[:, :, None] * inv_freq
    cos, sin = jnp.cos(theta), jnp.sin(theta)
    cos2 = jnp.concatenate([cos, cos], axis=-1)
    sin2 = jnp.concatenate([-sin, sin], axis=-1)
    o_c = _ret_mixer(y, cos2, sin2)
    w_up_pad = jnp.pad(gla_w_up.astype(F32), ((0, LANES - GLA_RANK), (0, 0)))
    o_d = _gla_mixer(y, yla, w_up_pad, gla_b_up.reshape(1, -1).astype(F32),
                     gla_norm_w.reshape(1, GLA_DV).astype(F32))
    mixed = jnp.concatenate([o_c, o_d], axis=-1).reshape(b * t, d)
    return _matmul(mixed, w_out.astype(BF16), out_dtype=F32, tm=1024, tn=1024, tk=d, name="odd_out_proj")


def _mlp(xb2, w_up, w_down, layer):
    hdn = _matmul(xb2, w_up.astype(BF16), out_dtype=BF16, tm=1024, tn=1024, tk=D_MODEL, relu2=True,
                  name=f"mlp_up_{layer}")
    return _matmul(hdn, w_down.astype(BF16), out_dtype=F32, tm=1024, tn=1024, tk=2048, name=f"mlp_down_{layer}")


def kernel(x, positions, e_w_in, e_conv_w, e_a_log, e_dt_bias, e_gdn_norm_w, e_mlstm_gate_b, e_mlstm_norm_w, e_w_out, o_w_in, o_gla_w_up, o_gla_b_up, o_gla_norm_w, o_w_out, ln_mix_g, ln_mix_b, mlp_w_up, mlp_w_down, ln_mlp_g, ln_mlp_b):
    b, t, d = x.shape
    x2 = x.reshape(b * t, d)
    xb2 = x2.astype(BF16)
    for layer in range(DEPTH):
        j = layer // 2
        xb = xb2.reshape(b, t, d)
        if layer % 2 == 0:
            h = _even_mixer(xb, e_w_in[j], e_conv_w[j], e_a_log[j], e_dt_bias[j], e_gdn_norm_w[j],
                            e_mlstm_gate_b[j], e_mlstm_norm_w[j], e_w_out[j])
        else:
            h = _odd_mixer(xb, positions, o_w_in[j], o_gla_w_up[j], o_gla_b_up[j], o_gla_norm_w[j], o_w_out[j])
        x2, xb2 = _ln_residual(x2, h, ln_mix_g[layer], ln_mix_b[layer], name=f"ln_mix_{layer}")
        f = _mlp(xb2, mlp_w_up[layer], mlp_w_down[layer], layer)
        x2, xb2 = _ln_residual(x2, f, ln_mlp_g[layer], ln_mlp_b[layer], name=f"ln_mlp_{layer}")
    return x2.reshape(b, t, d)
```

```python
import functools
import math

import jax
import jax.numpy as jnp
from jax import lax
from jax.experimental import pallas as pl
from jax.experimental.pallas import tpu as pltpu

F32 = jnp.float32
BF16 = jnp.bfloat16
HIGHEST = lax.Precision.HIGHEST

D_MODEL = 4096
DEPTH = 2
CHUNK = 64
D_FF = 4 * D_MODEL
GROUP_WIDTH = D_MODEL // 2
ALPHA = (2.0 * DEPTH) ** 0.25
EPS = 1e-6
LN_EPS = 1e-5

GDN_HEADS = 16
GDN_D = GROUP_WIDTH // GDN_HEADS
CONV_K = 4
GDN_QKV = 3 * GROUP_WIDTH
ML_HEADS = 8
ML_DV = GROUP_WIDTH // ML_HEADS
ML_DK = ML_DV // 2
RET_HEADS = 8
RET_DV = GROUP_WIDTH // RET_HEADS
RET_DK = RET_DV // 2
ROPE_BASE = 10000.0
GLA_HEADS = 4
GLA_DV = GROUP_WIDTH // GLA_HEADS
GLA_DK = GLA_DV // 2
GLA_RANK = 16
GLA_TAU = 16.0

STACK = 4
ROWS = STACK * CHUNK
GDN_BLOCK = 2 * CHUNK
MIX_BLOCK = 4 * CHUNK
LANES = 128
EVEN_MAIN = GDN_QKV + GROUP_WIDTH + 2 * ML_HEADS * ML_DK + 2 * GROUP_WIDTH
ODD_MAIN = 2 * RET_HEADS * RET_DK + 2 * GROUP_WIDTH + 2 * GLA_HEADS * GLA_DK + 2 * GROUP_WIDTH
GA_OFF, GB_OFF, MI_OFF, MF_OFF = 0, GDN_HEADS, 2 * GDN_HEADS, 2 * GDN_HEADS + ML_HEADS

VMEM_LIMIT = 56 * 1024 * 1024


def _sigmoid(x):
    return 1.0 / (1.0 + jnp.exp(-x))


def _softplus(x):
    return jnp.maximum(x, 0.0) + jnp.log1p(jnp.exp(-jnp.abs(x)))


def _log_sigmoid(x):
    return -_softplus(-x)


def _dot(a, b):
    return jnp.dot(a, b, preferred_element_type=F32)


def _dot_nt(a, b):
    return lax.dot_general(a, b, (((1,), (1,)), ((), ())), preferred_element_type=F32)


def _dot_tn(a, b):
    return _dot(a.T, b)


def _chunk_masks():
    row = lax.broadcasted_iota(jnp.int32, (CHUNK, CHUNK), 0)
    col = lax.broadcasted_iota(jnp.int32, (CHUNK, CHUNK), 1)
    return row, col


def _cumsum_rows(x, row, col):
    tril = jnp.where(row >= col, 1.0, 0.0).astype(F32)
    return jnp.dot(tril, x, precision=HIGHEST, preferred_element_type=F32)


def _unit_lower_inverses(mats, row, col):
    eye = jnp.where(row == col, 1.0, 0.0).astype(F32)
    same = (row >> 1) == (col >> 1)
    invs = [eye - jnp.where(same, a, 0.0) for a in mats]
    shift = 2
    while (1 << (shift - 1)) < CHUNK:
        same2 = (row >> shift) == (col >> shift)
        off_mask = jnp.logical_and(same2, jnp.logical_not(same))
        inv_b = [inv.astype(BF16) for inv in invs]
        tmp = [_dot(jnp.where(off_mask, a, 0.0).astype(BF16), ib) for a, ib in zip(mats, inv_b)]
        invs = [inv - _dot(ib, t.astype(BF16)) for inv, ib, t in zip(invs, inv_b, tmp)]
        same = same2
        shift += 1
    return invs


def _mm_kernel_single(a_ref, b_ref, o_ref, *, relu2):
    r = _dot(a_ref[...], b_ref[...])
    if relu2:
        r = jnp.square(jnp.maximum(r, 0.0))
    o_ref[...] = r.astype(o_ref.dtype)


def _mm_kernel_acc(a_ref, b_ref, o_ref, acc_ref, *, nk, relu2):
    k = pl.program_id(2)

    @pl.when(k == 0)
    def _():
        acc_ref[...] = jnp.zeros_like(acc_ref)

    acc_ref[...] += _dot(a_ref[...], b_ref[...])

    @pl.when(k == nk - 1)
    def _():
        r = acc_ref[...]
        if relu2:
            r = jnp.square(jnp.maximum(r, 0.0))
        o_ref[...] = r.astype(o_ref.dtype)


def _matmul(a, b, *, out_dtype, tm, tn, tk, relu2=False, name):
    m, k = a.shape
    _, n = b.shape
    tm, tn, tk = min(tm, m), min(tn, n), min(tk, k)
    assert m % tm == 0 and n % tn == 0 and k % tk == 0
    nk = k // tk
    if nk == 1:
        return pl.pallas_call(
            functools.partial(_mm_kernel_single, relu2=relu2),
            grid=(m // tm, n // tn),
            in_specs=[pl.BlockSpec((tm, k), lambda i, j: (i, 0)),
                      pl.BlockSpec((k, tn), lambda i, j: (0, j))],
            out_specs=pl.BlockSpec((tm, tn), lambda i, j: (i, j)),
            out_shape=jax.ShapeDtypeStruct((m, n), out_dtype),
            compiler_params=pltpu.CompilerParams(
                dimension_semantics=("parallel", "parallel"), vmem_limit_bytes=VMEM_LIMIT),
            name=name,
        )(a, b)
    return pl.pallas_call(
        functools.partial(_mm_kernel_acc, nk=nk, relu2=relu2),
        grid=(m // tm, n // tn, nk),
        in_specs=[pl.BlockSpec((tm, tk), lambda i, j, kk: (i, kk)),
                  pl.BlockSpec((tk, tn), lambda i, j, kk: (kk, j))],
        out_specs=pl.BlockSpec((tm, tn), lambda i, j, kk: (i, j)),
        out_shape=jax.ShapeDtypeStruct((m, n), out_dtype),
        scratch_shapes=[pltpu.VMEM((tm, tn), F32)],
        compiler_params=pltpu.CompilerParams(
            dimension_semantics=("parallel", "parallel", "arbitrary"), vmem_limit_bytes=VMEM_LIMIT),
        name=name,
    )(a, b)


def _mm_pair_kernel(a1_ref, a2_ref, b_ref, o_ref):
    k1 = a1_ref.shape[1]
    r = _dot(a1_ref[...], b_ref[0:k1, :]) + _dot(a2_ref[...], b_ref[k1:, :])
    o_ref[...] = r.astype(o_ref.dtype)


def _matmul_pair(a1, a2, b, *, out_dtype, tm, tn, name):
    m, k1 = a1.shape
    k2 = a2.shape[1]
    n = b.shape[1]
    tm, tn = min(tm, m), min(tn, n)
    assert m % tm == 0 and n % tn == 0 and b.shape[0] == k1 + k2
    return pl.pallas_call(
        _mm_pair_kernel,
        grid=(m // tm, n // tn),
        in_specs=[pl.BlockSpec((tm, k1), lambda i, j: (i, 0)),
                  pl.BlockSpec((tm, k2), lambda i, j: (i, 0)),
                  pl.BlockSpec((k1 + k2, tn), lambda i, j: (0, j))],
        out_specs=pl.BlockSpec((tm, tn), lambda i, j: (i, j)),
        out_shape=jax.ShapeDtypeStruct((m, n), out_dtype),
        compiler_params=pltpu.CompilerParams(
            dimension_semantics=("parallel", "parallel"), vmem_limit_bytes=VMEM_LIMIT),
        name=name,
    )(a1, a2, b)


def _ln_kernel(x_ref, h_ref, g_ref, b_ref, o_ref, ob_ref):
    t = ALPHA * x_ref[...] + h_ref[...]
    mu = jnp.mean(t, axis=-1, keepdims=True)
    d = t - mu
    var = jnp.mean(d * d, axis=-1, keepdims=True)
    r = d * lax.rsqrt(var + LN_EPS) * g_ref[...] + b_ref[...]
    o_ref[...] = r
    ob_ref[...] = r.astype(BF16)


def _ln_residual(x, h, g, b, *, tm=256, name):
    m, d = x.shape
    tm = min(tm, m)
    row = pl.BlockSpec((tm, d), lambda i: (i, 0))
    vec = pl.BlockSpec((1, d), lambda i: (0, 0))
    return pl.pallas_call(
        _ln_kernel,
        grid=(m // tm,),
        in_specs=[row, row, vec, vec],
        out_specs=[row, row],
        out_shape=[jax.ShapeDtypeStruct((m, d), F32), jax.ShapeDtypeStruct((m, d), BF16)],
        compiler_params=pltpu.CompilerParams(
            dimension_semantics=("parallel",), vmem_limit_bytes=VMEM_LIMIT),
        name=name,
    )(x, h, g.reshape(1, d), b.reshape(1, d))


def _stack_masks():
    srow = lax.broadcasted_iota(jnp.int32, (ROWS, ROWS), 0)
    scol = lax.broadcasted_iota(jnp.int32, (ROWS, ROWS), 1)
    same_head = (srow // CHUNK) == (scol // CHUNK)
    causal = jnp.logical_and(same_head, srow >= scol)
    strict = jnp.logical_and(same_head, srow > scol)
    pick = (lax.broadcasted_iota(jnp.int32, (STACK, ROWS), 1) // CHUNK
            == lax.broadcasted_iota(jnp.int32, (STACK, ROWS), 0))
    return srow, scol, causal, strict, pick


def _stack(fn, heads):
    return jnp.concatenate([fn(h) for h in heads], axis=0)


def _stack_bcast(fn, heads):
    return jnp.concatenate([jnp.broadcast_to(fn(h), (CHUNK, fn(h).shape[1])) for h in heads], axis=0)


def _row_form(t_rows, pick):
    return jnp.sum(jnp.where(pick, jnp.concatenate([t_rows] * STACK, axis=1), 0.0), axis=0, keepdims=True)


def _head_rows(j):
    return slice(j * CHUNK, (j + 1) * CHUNK)


def _chunk_rows(ci):
    return pl.ds(pl.multiple_of(ci * CHUNK, CHUNK), CHUNK)


def _mixer_call(kernel, y_specs, extra_specs, scratch, args, b, t, tb, name):
    return pl.pallas_call(
        kernel,
        grid=(b, t // tb),
        in_specs=y_specs + extra_specs,
        out_specs=pl.BlockSpec((1, tb, GROUP_WIDTH), lambda i, c: (i, c, 0)),
        out_shape=jax.ShapeDtypeStruct((b, t, GROUP_WIDTH), BF16),
        scratch_shapes=scratch,
        compiler_params=pltpu.CompilerParams(
            dimension_semantics=("arbitrary", "arbitrary"), vmem_limit_bytes=VMEM_LIMIT),
        name=name,
    )(*args)


def _tok_spec(tb, width, blk):
    return pl.BlockSpec((1, tb, width), lambda i, c: (i, c, blk))


def _full_spec(shape):
    return pl.BlockSpec(shape, lambda i, c: (0,) * len(shape))


def _gdn_kernel(qkv_ref, z_ref, gate_ref, convw_ref, alog_ref, dtb_ref, normw_ref, o_ref,
                cbuf, cs_ref, s_ref, *, tb):
    hist = 8

    @pl.when(pl.program_id(1) == 0)
    def _():
        cbuf[0:hist, :] = jnp.zeros((hist, GDN_QKV), F32)
        s_ref[...] = jnp.zeros_like(s_ref)

    cbuf[hist:hist + tb, :] = qkv_ref[0]
    for off in range(0, GDN_QKV, LANES):
        acc = cbuf[hist:hist + tb, off:off + LANES] * convw_ref[CONV_K - 1:CONV_K, off:off + LANES]
        for s in range(1, CONV_K):
            acc = acc + (cbuf[hist - s:hist - s + tb, off:off + LANES]
                         * convw_ref[CONV_K - 1 - s:CONV_K - s, off:off + LANES])
        cs_ref[:, off:off + LANES] = acc * _sigmoid(acc)
    cbuf[0:hist, :] = cbuf[tb:tb + hist, :]

    groups = [range(g * STACK, (g + 1) * STACK) for g in range(GDN_HEADS // STACK)]

    def chunk(ci, carry):
        tok = _chunk_rows(ci)
        row, col = _chunk_masks()
        srow, scol, causal, strict, pick = _stack_masks()
        gates = gate_ref[0, tok, :]
        g_all = -jnp.exp(alog_ref[...]) * _softplus(gates + dtb_ref[...])
        beta_all = _sigmoid(gates)
        gam_all = _cumsum_rows(g_all, row, col)
        gam_t = gam_all.T

        qs, ks, vs, gam_cs, beta_cs, decays, a_mats = [], [], [], [], [], [], []
        for heads in groups:
            q = _stack(lambda h: cs_ref[tok, h * GDN_D:(h + 1) * GDN_D], heads)
            k = _stack(lambda h: cs_ref[tok, GROUP_WIDTH + h * GDN_D:GROUP_WIDTH + (h + 1) * GDN_D], heads)
            v = _stack(lambda h: cs_ref[tok, 2 * GROUP_WIDTH + h * GDN_D:2 * GROUP_WIDTH + (h + 1) * GDN_D], heads)
            q = q * (lax.rsqrt(jnp.sum(q * q, axis=-1, keepdims=True) + EPS) * GDN_D ** -0.5)
            k = k * lax.rsqrt(jnp.sum(k * k, axis=-1, keepdims=True) + EPS)
            gam_c = _stack(lambda h: gam_all[:, GA_OFF + h:GA_OFF + h + 1], heads)
            beta_c = _stack(lambda h: beta_all[:, GB_OFF + h:GB_OFF + h + 1], heads)
            gam_r = _row_form(gam_t[GA_OFF + heads[0]:GA_OFF + heads[0] + STACK, :], pick)
            decay = jnp.where(causal, jnp.exp(jnp.where(causal, gam_c - gam_r, 0.0)), 0.0)
            kb = k.astype(BF16)
            a_mats.append(jnp.where(strict, _dot_nt(kb, kb) * decay * beta_c, 0.0))
            qs.append(q); ks.append(k); vs.append(v)
            gam_cs.append(gam_c); beta_cs.append(beta_c); decays.append(decay)

        invs = _unit_lower_inverses(a_mats, srow, scol)

        for g, heads in enumerate(groups):
            q, k, v, gam_c, beta_c = qs[g], ks[g], vs[g], gam_cs[g], beta_cs[g]
            egam = jnp.exp(gam_c)
            rhs = jnp.concatenate([v * beta_c, k * (beta_c * egam)], axis=1)
            sol = _dot(invs[g], rhs)
            qk = _dot_nt(q, k) * decays[g]
            q_dec = q * egam
            v_new, q_state, states = [], [], []
            for j, h in enumerate(heads):
                sl = _head_rows(j)
                state = s_ref[h]
                both = _dot(jnp.concatenate([sol[sl, GDN_D:], q_dec[sl]], axis=0), state)
                v_new.append(sol[sl, :GDN_D] - both[:CHUNK])
                q_state.append(both[CHUNK:])
                states.append(state)
            o = jnp.concatenate(q_state, axis=0) + _dot(qk, jnp.concatenate(v_new, axis=0))
            gam_last = _stack_bcast(lambda h: gam_all[CHUNK - 1:CHUNK, GA_OFF + h:GA_OFF + h + 1], heads)
            k_end = k * jnp.exp(gam_last - gam_c)
            for j, h in enumerate(heads):
                c_dec = jnp.exp(gam_all[CHUNK - 1:CHUNK, GA_OFF + h:GA_OFF + h + 1])
                s_ref[h] = states[j] * c_dec + _dot_tn(k_end[_head_rows(j)], v_new[j])
            o = o * lax.rsqrt(jnp.mean(o * o, axis=-1, keepdims=True) + EPS) * normw_ref[...]
            for j, h in enumerate(heads):
                zz = z_ref[0, tok, h * GDN_D:(h + 1) * GDN_D]
                o_ref[0, tok, h * GDN_D:(h + 1) * GDN_D] = (
                    o[_head_rows(j)] * (zz * _sigmoid(zz))).astype(o_ref.dtype)
        return carry

    lax.fori_loop(0, tb // CHUNK, chunk, 0)


def _gdn_mixer(y, yg, conv_w, alog_row, dtb_row, norm_w):
    b, t, _ = y.shape
    tb = min(GDN_BLOCK, t)
    return _mixer_call(
        functools.partial(_gdn_kernel, tb=tb),
        [_tok_spec(tb, GDN_QKV, 0), _tok_spec(tb, GROUP_WIDTH, GDN_QKV // GROUP_WIDTH), _tok_spec(tb, LANES, 0)],
        [_full_spec((CONV_K, GDN_QKV)), _full_spec((1, LANES)), _full_spec((1, LANES)), _full_spec((1, GDN_D))],
        [pltpu.VMEM((tb + 8, GDN_QKV), F32), pltpu.VMEM((tb, GDN_QKV), F32),
         pltpu.VMEM((GDN_HEADS, GDN_D, GDN_D), F32)],
        (y, y, yg, conv_w, alog_row, dtb_row, norm_w), b, t, tb, "gdn_mixer")


def _mlstm_kernel(q_ref, k_ref, v_ref, og_ref, gate_ref, gb_ref, normw_ref, o_ref, c_ref, n_ref, m_ref, *, tb):
    @pl.when(pl.program_id(1) == 0)
    def _():
        c_ref[...] = jnp.zeros_like(c_ref)
        n_ref[...] = jnp.zeros_like(n_ref)
        m_ref[...] = jnp.zeros_like(m_ref)

    groups = [range(g * STACK, (g + 1) * STACK) for g in range(ML_HEADS // STACK)]

    def chunk(ci, carry):
        tok = _chunk_rows(ci)
        row, col = _chunk_masks()
        _, _, causal, _, pick = _stack_masks()
        pre = gate_ref[0, tok, :] + gb_ref[...]
        fc_all = _cumsum_rows(_log_sigmoid(pre), row, col)
        fc_t = fc_all.T
        pre_t = pre.T

        st = []
        for heads in groups:
            h0 = heads[0]
            q = _stack(lambda h: q_ref[0, tok, h * ML_DK:(h + 1) * ML_DK], heads)
            k = _stack(lambda h: k_ref[0, tok, h * ML_DK:(h + 1) * ML_DK], heads) * ML_DK ** -0.5
            v = _stack(lambda h: v_ref[0, tok, h * ML_DV:(h + 1) * ML_DV], heads)
            fc_c = _stack(lambda h: fc_all[:, MF_OFF + h:MF_OFF + h + 1], heads)
            ic_c = _stack(lambda h: pre[:, MI_OFF + h:MI_OFF + h + 1], heads)
            fc_last = _stack_bcast(lambda h: fc_all[CHUNK - 1:CHUNK, MF_OFF + h:MF_OFF + h + 1], heads)
            fc_r = _row_form(fc_t[MF_OFF + h0:MF_OFF + h0 + STACK, :], pick)
            ic_r = _row_form(pre_t[MI_OFF + h0:MI_OFF + h0 + STACK, :], pick)
            d_log = jnp.where(causal, fc_c - fc_r + ic_r, -jnp.inf)
            m_intra = jnp.max(d_log, axis=-1, keepdims=True)
            qk = _dot_nt(q, k)
            src_end = fc_last - fc_c + ic_c
            m_src = jnp.concatenate(
                [jnp.broadcast_to(jnp.max(src_end[_head_rows(j)], axis=0, keepdims=True), (CHUNK, 1))
                 for j in range(STACK)], axis=0)
            m_st = _stack_bcast(lambda h: m_ref[h, 0:1, 0:1], heads)
            n_st = _stack_bcast(lambda h: n_ref[h, 0:1, :], heads)
            c_sts = [c_ref[h] for h in heads]
            m_inter = fc_c + m_st
            m_t = jnp.maximum(m_inter, m_intra)
            w_inter = jnp.exp(m_inter - m_t)
            w_intra = jnp.exp(d_log - m_t) * qk
            q_c = jnp.concatenate([_dot(q[_head_rows(j)], c_sts[j]) for j in range(STACK)], axis=0)
            num = w_inter * q_c + _dot(w_intra, v)
            den = (w_inter * jnp.sum(q * n_st, axis=-1, keepdims=True)
                   + jnp.sum(w_intra, axis=-1, keepdims=True))
            hh = num / jnp.maximum(jnp.abs(den), jnp.exp(-m_t))
            m_new = jnp.maximum(fc_last + m_st, m_src)
            w_state = jnp.exp(fc_last + m_st - m_new)
            kw = k * jnp.exp(src_end - m_new)
            st.append((heads, v, hh, m_new, w_state, kw, c_sts, n_st))

        for heads, v, hh, m_new, w_state, kw, c_sts, n_st in st:
            for j, h in enumerate(heads):
                sl = _head_rows(j)
                ws = w_state[j * CHUNK:j * CHUNK + 1, :]
                c_ref[h] = ws * c_sts[j] + _dot_tn(kw[sl], v[sl])
                n_ref[h] = jnp.broadcast_to(
                    ws * n_st[j * CHUNK:j * CHUNK + 1, :] + jnp.sum(kw[sl], axis=0, keepdims=True), (8, ML_DK))
                m_ref[h] = jnp.broadcast_to(m_new[j * CHUNK:j * CHUNK + 1, :], (8, LANES))
            hh = hh * lax.rsqrt(jnp.mean(hh * hh, axis=-1, keepdims=True) + EPS) * normw_ref[...]
            for j, h in enumerate(heads):
                gg = og_ref[0, tok, h * ML_DV:(h + 1) * ML_DV]
                o_ref[0, tok, h * ML_DV:(h + 1) * ML_DV] = (hh[_head_rows(j)] * _sigmoid(gg)).astype(o_ref.dtype)
        return carry

    lax.fori_loop(0, tb // CHUNK, chunk, 0)


def _mlstm_mixer(y, yg, gb_row, norm_w):
    b, t, _ = y.shape
    tb = min(MIX_BLOCK, t)
    qk_w = ML_HEADS * ML_DK
    base = GDN_QKV + GROUP_WIDTH
    vblk = (base + 2 * qk_w) // GROUP_WIDTH
    return _mixer_call(
        functools.partial(_mlstm_kernel, tb=tb),
        [_tok_spec(tb, qk_w, base // qk_w), _tok_spec(tb, qk_w, base // qk_w + 1),
         _tok_spec(tb, GROUP_WIDTH, vblk), _tok_spec(tb, GROUP_WIDTH, vblk + 1), _tok_spec(tb, LANES, 0)],
        [_full_spec((1, LANES)), _full_spec((1, ML_DV))],
        [pltpu.VMEM((ML_HEADS, ML_DK, ML_DV), F32), pltpu.VMEM((ML_HEADS, 8, ML_DK), F32),
         pltpu.VMEM((ML_HEADS, 8, LANES), F32)],
        (y, y, y, y, yg, gb_row, norm_w), b, t, tb, "mlstm_mixer")


def _gla_group(q, k, v, bcum, bcum_last, end_decay_cols, states, causal):
    q_dec = q * jnp.exp(bcum)
    k_inv = k * jnp.exp(-bcum)
    attn = jnp.where(causal, _dot_nt(q_dec, k_inv), 0.0)
    o = _dot(attn, v) + jnp.concatenate(
        [_dot(q_dec[_head_rows(j)], states[j]) for j in range(STACK)], axis=0)
    k_end = k * jnp.exp(bcum_last - bcum)
    new_states = [states[j] * end_decay_cols[j] + _dot_tn(k_end[_head_rows(j)], v[_head_rows(j)])
                  for j in range(STACK)]
    return o, new_states


def _ret_kernel(q_ref, k_ref, v_ref, g_ref, cos_ref, sin_ref, o_ref, s_ref, *, tb):
    @pl.when(pl.program_id(1) == 0)
    def _():
        s_ref[...] = jnp.zeros_like(s_ref)

    groups = [range(g * STACK, (g + 1) * STACK) for g in range(RET_HEADS // STACK)]

    def chunk(ci, carry):
        tok = _chunk_rows(ci)
        _, _, causal, _, _ = _stack_masks()
        cos2 = jnp.concatenate([cos_ref[0, tok, :]] * STACK, axis=0)
        sin2 = jnp.concatenate([sin_ref[0, tok, :]] * STACK, axis=0)
        steps = (lax.broadcasted_iota(jnp.int32, (CHUNK, 1), 0) + 1).astype(F32)
        for heads in groups:
            log_gamma = [math.log(1.0 - 2.0 ** (-5.0 - h)) for h in heads]
            q = _stack(lambda h: q_ref[0, tok, h * RET_DK:(h + 1) * RET_DK], heads)
            k = _stack(lambda h: k_ref[0, tok, h * RET_DK:(h + 1) * RET_DK], heads)
            v = _stack(lambda h: v_ref[0, tok, h * RET_DV:(h + 1) * RET_DV], heads)
            q = q * cos2 + pltpu.roll(q, RET_DK // 2, 1) * sin2
            k = (k * cos2 + pltpu.roll(k, RET_DK // 2, 1) * sin2) * RET_DK ** -0.5
            bcum = jnp.concatenate([steps * lg for lg in log_gamma], axis=0)
            bcum_last = jnp.concatenate([jnp.full((CHUNK, 1), CHUNK * lg, F32) for lg in log_gamma], axis=0)
            ends = [jnp.exp(jnp.full((1, 1), CHUNK * lg, F32)) for lg in log_gamma]
            o, new_states = _gla_group(q, k, v, bcum, bcum_last, ends, [s_ref[h] for h in heads], causal)
            for j, h in enumerate(heads):
                s_ref[h] = new_states[j]
            o = o * lax.rsqrt(jnp.mean(o * o, axis=-1, keepdims=True) + EPS)
            for j, h in enumerate(heads):
                gg = g_ref[0, tok, h * RET_DV:(h + 1) * RET_DV]
                o_ref[0, tok, h * RET_DV:(h + 1) * RET_DV] = (
                    o[_head_rows(j)] * (gg * _sigmoid(gg))).astype(o_ref.dtype)
        return carry

    lax.fori_loop(0, tb // CHUNK, chunk, 0)


def _ret_mixer(y, cos2, sin2):
    b, t, _ = y.shape
    tb = min(MIX_BLOCK, t)
    qk_w = RET_HEADS * RET_DK
    return _mixer_call(
        functools.partial(_ret_kernel, tb=tb),
        [_tok_spec(tb, qk_w, 0), _tok_spec(tb, qk_w, 1), _tok_spec(tb, GROUP_WIDTH, 1),
         _tok_spec(tb, GROUP_WIDTH, 2), _tok_spec(tb, RET_DK, 0), _tok_spec(tb, RET_DK, 0)],
        [],
        [pltpu.VMEM((RET_HEADS, RET_DK, RET_DV), F32)],
        (y, y, y, y, cos2, sin2), b, t, tb, "ret_mixer")


def _gla_kernel(q_ref, k_ref, v_ref, r_ref, la_ref, wup_ref, bup_ref, normw_ref, o_ref, s_ref, *, tb):
    @pl.when(pl.program_id(1) == 0)
    def _():
        s_ref[...] = jnp.zeros_like(s_ref)

    heads = range(GLA_HEADS)
    assert GLA_HEADS == STACK

    def chunk(ci, carry):
        tok = _chunk_rows(ci)
        row, col = _chunk_masks()
        _, _, causal, _, _ = _stack_masks()
        gate = _dot(la_ref[0, tok, :], wup_ref[...]) + bup_ref[...]
        bcum_all = _cumsum_rows(_log_sigmoid(gate) * (1.0 / GLA_TAU), row, col)
        q = _stack(lambda h: q_ref[0, tok, h * GLA_DK:(h + 1) * GLA_DK], heads) * GLA_DK ** -0.5
        k = _stack(lambda h: k_ref[0, tok, h * GLA_DK:(h + 1) * GLA_DK], heads)
        v = _stack(lambda h: v_ref[0, tok, h * GLA_DV:(h + 1) * GLA_DV], heads)
        bcum = _stack(lambda h: bcum_all[:, h * GLA_DK:(h + 1) * GLA_DK], heads)
        bcum_last = _stack_bcast(lambda h: bcum_all[CHUNK - 1:CHUNK, h * GLA_DK:(h + 1) * GLA_DK], heads)
        ends = [jnp.exp(bcum_all[:, h * GLA_DK:(h + 1) * GLA_DK].T[:, CHUNK - 1:CHUNK]) for h in heads]
        o, new_states = _gla_group(q, k, v, bcum, bcum_last, ends, [s_ref[h] for h in heads], causal)
        for h in heads:
            s_ref[h] = new_states[h]
        o = o * lax.rsqrt(jnp.mean(o * o, axis=-1, keepdims=True) + EPS) * normw_ref[...]
        for h in heads:
            rr = r_ref[0, tok, h * GLA_DV:(h + 1) * GLA_DV]
            o_ref[0, tok, h * GLA_DV:(h + 1) * GLA_DV] = (
                o[_head_rows(h)] * (rr * _sigmoid(rr))).astype(o_ref.dtype)
        return carry

    lax.fori_loop(0, tb // CHUNK, chunk, 0)


def _gla_mixer(y, yla, w_up_pad, b_up, norm_w):
    b, t, _ = y.shape
    tb = min(MIX_BLOCK, t)
    qk_w = GLA_HEADS * GLA_DK
    base = 2 * RET_HEADS * RET_DK + 2 * GROUP_WIDTH
    vblk = (base + 2 * qk_w) // GROUP_WIDTH
    return _mixer_call(
        functools.partial(_gla_kernel, tb=tb),
        [_tok_spec(tb, qk_w, base // qk_w), _tok_spec(tb, qk_w, base // qk_w + 1),
         _tok_spec(tb, GROUP_WIDTH, vblk), _tok_spec(tb, GROUP_WIDTH, vblk + 1), _tok_spec(tb, LANES, 0)],
        [_full_spec((LANES, qk_w)), _full_spec((1, qk_w)), _full_spec((1, GLA_DV))],
        [pltpu.VMEM((GLA_HEADS, GLA_DK, GLA_DV), F32)],
        (y, y, y, y, yla, w_up_pad, b_up, norm_w), b, t, tb, "gla_mixer")


def _pad_lanes(v, offset=0):
    return jnp.zeros((1, LANES), F32).at[0, offset:offset + v.shape[0]].set(v.astype(F32))


def _even_mixer(xb, w_in, conv_w, a_log, dt_bias, gdn_norm_w, ml_gate_b, ml_norm_w, w_out):
    b, t, d = xb.shape
    x2 = xb.reshape(b * t, d)
    g_end = GDN_QKV + GROUP_WIDTH
    m_start = g_end + 2 * GDN_HEADS
    m_end = m_start + EVEN_MAIN - g_end
    w_main = jnp.concatenate([w_in[:, :g_end], w_in[:, m_start:m_end]], axis=1).astype(BF16)
    w_gate = jnp.concatenate([w_in[:, g_end:m_start], w_in[:, m_end:]], axis=1)
    w_gate = jnp.pad(w_gate, ((0, 0), (0, LANES - w_gate.shape[1]))).astype(BF16)
    y = _matmul(x2, w_main, out_dtype=F32, tm=1024, tn=1024, tk=d, name="even_in_proj").reshape(b, t, EVEN_MAIN)
    yg = _matmul(x2, w_gate, out_dtype=F32, tm=1024, tn=LANES, tk=d, name="even_gate_proj").reshape(b, t, LANES)
    o_a = _gdn_mixer(y, yg, conv_w.astype(F32), _pad_lanes(a_log, GA_OFF), _pad_lanes(dt_bias, GA_OFF),
                     gdn_norm_w.reshape(1, GDN_D).astype(F32))
    o_b = _mlstm_mixer(y, yg, _pad_lanes(ml_gate_b, MI_OFF), ml_norm_w.reshape(1, ML_DV).astype(F32))
    return _matmul_pair(o_a.reshape(b * t, GROUP_WIDTH), o_b.reshape(b * t, GROUP_WIDTH), w_out.astype(BF16),
                        out_dtype=F32, tm=1024, tn=1024, name="even_out_proj")


def _odd_mixer(xb, positions, w_in, gla_w_up, gla_b_up, gla_norm_w, w_out):
    b, t, d = xb.shape
    x2 = xb.reshape(b * t, d)
    w_main = w_in[:, :ODD_MAIN].astype(BF16)
    w_la = jnp.pad(w_in[:, ODD_MAIN:], ((0, 0), (0, LANES - GLA_RANK))).astype(BF16)
    y = _matmul(x2, w_main, out_dtype=F32, tm=1024, tn=1024, tk=d, name="odd_in_proj").reshape(b, t, ODD_MAIN)
    yla = _matmul(x2, w_la, out_dtype=F32, tm=1024, tn=LANES, tk=d, name="odd_gate_proj").reshape(b, t, LANES)
    inv_freq = 1.0 / (ROPE_BASE ** jnp.linspace(0.0, 1.0, RET_DK // 2, dtype=F32))
    theta = positions.astype(F32)[:, :, None] * inv_freq
    cos, sin = jnp.cos(theta), jnp.sin(theta)
    cos2 = jnp.concatenate([cos, cos], axis=-1)
    sin2 = jnp.concatenate([-sin, sin], axis=-1)
    o_c = _ret_mixer(y, cos2, sin2)
    w_up_pad = jnp.pad(gla_w_up.astype(F32), ((0, LANES - GLA_RANK), (0, 0)))
    o_d = _gla_mixer(y, yla, w_up_pad, gla_b_up.reshape(1, -1).astype(F32),
                     gla_norm_w.reshape(1, GLA_DV).astype(F32))
    return _matmul_pair(o_c.reshape(b * t, GROUP_WIDTH), o_d.reshape(b * t, GROUP_WIDTH), w_out.astype(BF16),
                        out_dtype=F32, tm=1024, tn=1024, name="odd_out_proj")


def _mlp(xb2, w_up, w_down, layer):
    hdn = _matmul(xb2, w_up.astype(BF16), out_dtype=BF16, tm=1024, tn=1024, tk=D_MODEL, relu2=True,
                  name=f"mlp_up_{layer}")
    return _matmul(hdn, w_down.astype(BF16), out_dtype=F32, tm=1024, tn=1024, tk=2048, name=f"mlp_down_{layer}")


def kernel(x, positions, e_w_in, e_conv_w, e_a_log, e_dt_bias, e_gdn_norm_w, e_mlstm_gate_b, e_mlstm_norm_w, e_w_out, o_w_in, o_gla_w_up, o_gla_b_up, o_gla_norm_w, o_w_out, ln_mix_g, ln_mix_b, mlp_w_up, mlp_w_down, ln_mlp_g, ln_mlp_b):
    b, t, d = x.shape
    x2 = x.reshape(b * t, d)
    xb2 = x2.astype(BF16)
    for layer in range(DEPTH):
        j = layer // 2
        xb = xb2.reshape(b, t, d)
        if layer % 2 == 0:
            h = _even_mixer(xb, e_w_in[j], e_conv_w[j], e_a_log[j], e_dt_bias[j], e_gdn_norm_w[j],
                            e_mlstm_gate_b[j], e_mlstm_norm_w[j], e_w_out[j])
        else:
            h = _odd_mixer(xb, positions, o_w_in[j], o_gla_w_up[j], o_gla_b_up[j], o_gla_norm_w[j], o_w_out[j])
        x2, xb2 = _ln_residual(x2, h, ln_mix_g[layer], ln_mix_b[layer], name=f"ln_mix_{layer}")
        f = _mlp(xb2, mlp_w_up[layer], mlp_w_down[layer], layer)
        x2, xb2 = _ln_residual(x2, f, ln_mlp_g[layer], ln_mlp_b[layer], name=f"ln_mlp_{layer}")
    return x2.reshape(b, t, d)
```

```python
import functools
import math

import jax
import jax.numpy as jnp
from jax import lax
from jax.experimental import pallas as pl
from jax.experimental.pallas import tpu as pltpu

F32 = jnp.float32
BF16 = jnp.bfloat16
HIGHEST = lax.Precision.HIGHEST

D_MODEL = 4096
DEPTH = 2
CHUNK = 64
D_FF = 4 * D_MODEL
GROUP_WIDTH = D_MODEL // 2
ALPHA = (2.0 * DEPTH) ** 0.25
EPS = 1e-6
LN_EPS = 1e-5

GDN_HEADS = 16
GDN_D = GROUP_WIDTH // GDN_HEADS
CONV_K = 4
GDN_QKV = 3 * GROUP_WIDTH
ML_HEADS = 8
ML_DV = GROUP_WIDTH // ML_HEADS
ML_DK = ML_DV // 2
RET_HEADS = 8
RET_DV = GROUP_WIDTH // RET_HEADS
RET_DK = RET_DV // 2
ROPE_BASE = 10000.0
GLA_HEADS = 4
GLA_DV = GROUP_WIDTH // GLA_HEADS
GLA_DK = GLA_DV // 2
GLA_RANK = 16
GLA_TAU = 16.0

STACK = 4
ROWS = STACK * CHUNK
GDN_BLOCK = 2 * CHUNK
MIX_BLOCK = 4 * CHUNK
LANES = 128
EVEN_MAIN = GDN_QKV + GROUP_WIDTH + 2 * ML_HEADS * ML_DK + 2 * GROUP_WIDTH
ODD_MAIN = 2 * RET_HEADS * RET_DK + 2 * GROUP_WIDTH + 2 * GLA_HEADS * GLA_DK + 2 * GROUP_WIDTH
GA_OFF, GB_OFF, MI_OFF, MF_OFF = 0, GDN_HEADS, 2 * GDN_HEADS, 2 * GDN_HEADS + ML_HEADS

VMEM_LIMIT = 56 * 1024 * 1024
CAST_BLOCK_BYTES = 8 * 1024 * 1024


def _sigmoid(x):
    return 1.0 / (1.0 + jnp.exp(-x))


def _softplus(x):
    return jnp.maximum(x, 0.0) + jnp.log1p(jnp.exp(-jnp.abs(x)))


def _log_sigmoid(x):
    return -_softplus(-x)


def _mxu(x):
    return x.astype(BF16)


def _dot(a, b):
    return jnp.dot(_mxu(a), _mxu(b), preferred_element_type=F32)


def _dot_nt(a, b):
    return lax.dot_general(_mxu(a), _mxu(b), (((1,), (1,)), ((), ())), preferred_element_type=F32)


def _dot_tn(a, b):
    return _dot(a.T, b)


def _chunk_masks():
    row = lax.broadcasted_iota(jnp.int32, (CHUNK, CHUNK), 0)
    col = lax.broadcasted_iota(jnp.int32, (CHUNK, CHUNK), 1)
    return row, col


def _cumsum_rows(x, row, col):
    tril = jnp.where(row >= col, 1.0, 0.0).astype(F32)
    return jnp.dot(tril, x, precision=HIGHEST, preferred_element_type=F32)


def _unit_lower_inverses(mats, row, col):
    eye = jnp.where(row == col, 1.0, 0.0).astype(F32)
    same = (row >> 1) == (col >> 1)
    invs = [eye - jnp.where(same, a, 0.0) for a in mats]
    shift = 2
    while (1 << (shift - 1)) < CHUNK:
        same2 = (row >> shift) == (col >> shift)
        off_mask = jnp.logical_and(same2, jnp.logical_not(same))
        inv_b = [_mxu(inv) for inv in invs]
        tmp = [_dot(jnp.where(off_mask, a, 0.0), ib) for a, ib in zip(mats, inv_b)]
        invs = [inv - _dot(ib, t) for inv, ib, t in zip(invs, inv_b, tmp)]
        same = same2
        shift += 1
    return invs


def _mm_kernel_single(a_ref, b_ref, o_ref, *, relu2):
    r = _dot(a_ref[...], b_ref[...])
    if relu2:
        r = jnp.square(jnp.maximum(r, 0.0))
    o_ref[...] = r.astype(o_ref.dtype)


def _mm_kernel_acc(a_ref, b_ref, o_ref, acc_ref, *, nk, relu2):
    k = pl.program_id(2)

    @pl.when(k == 0)
    def _():
        acc_ref[...] = jnp.zeros_like(acc_ref)

    acc_ref[...] += _dot(a_ref[...], b_ref[...])

    @pl.when(k == nk - 1)
    def _():
        r = acc_ref[...]
        if relu2:
            r = jnp.square(jnp.maximum(r, 0.0))
        o_ref[...] = r.astype(o_ref.dtype)


def _matmul(a, b, *, out_dtype, tm, tn, tk, relu2=False, name):
    m, k = a.shape
    _, n = b.shape
    tm, tn, tk = min(tm, m), min(tn, n), min(tk, k)
    assert m % tm == 0 and n % tn == 0 and k % tk == 0
    nk = k // tk
    if nk == 1:
        return pl.pallas_call(
            functools.partial(_mm_kernel_single, relu2=relu2),
            grid=(m // tm, n // tn),
            in_specs=[pl.BlockSpec((tm, k), lambda i, j: (i, 0)),
                      pl.BlockSpec((k, tn), lambda i, j: (0, j))],
            out_specs=pl.BlockSpec((tm, tn), lambda i, j: (i, j)),
            out_shape=jax.ShapeDtypeStruct((m, n), out_dtype),
            compiler_params=pltpu.CompilerParams(
                dimension_semantics=("parallel", "parallel"), vmem_limit_bytes=VMEM_LIMIT),
            name=name,
        )(a, b)
    return pl.pallas_call(
        functools.partial(_mm_kernel_acc, nk=nk, relu2=relu2),
        grid=(m // tm, n // tn, nk),
        in_specs=[pl.BlockSpec((tm, tk), lambda i, j, kk: (i, kk)),
                  pl.BlockSpec((tk, tn), lambda i, j, kk: (kk, j))],
        out_specs=pl.BlockSpec((tm, tn), lambda i, j, kk: (i, j)),
        out_shape=jax.ShapeDtypeStruct((m, n), out_dtype),
        scratch_shapes=[pltpu.VMEM((tm, tn), F32)],
        compiler_params=pltpu.CompilerParams(
            dimension_semantics=("parallel", "parallel", "arbitrary"), vmem_limit_bytes=VMEM_LIMIT),
        name=name,
    )(a, b)


def _mm_pair_kernel(a1_ref, a2_ref, b_ref, o_ref):
    k1 = a1_ref.shape[1]
    r = _dot(a1_ref[...], b_ref[0:k1, :]) + _dot(a2_ref[...], b_ref[k1:, :])
    o_ref[...] = r.astype(o_ref.dtype)


def _matmul_pair(a1, a2, b, *, out_dtype, tm, tn, name):
    m, k1 = a1.shape
    k2 = a2.shape[1]
    n = b.shape[1]
    tm, tn = min(tm, m), min(tn, n)
    assert m % tm == 0 and n % tn == 0 and b.shape[0] == k1 + k2
    return pl.pallas_call(
        _mm_pair_kernel,
        grid=(m // tm, n // tn),
        in_specs=[pl.BlockSpec((tm, k1), lambda i, j: (i, 0)),
                  pl.BlockSpec((tm, k2), lambda i, j: (i, 0)),
                  pl.BlockSpec((k1 + k2, tn), lambda i, j: (0, j))],
        out_specs=pl.BlockSpec((tm, tn), lambda i, j: (i, j)),
        out_shape=jax.ShapeDtypeStruct((m, n), out_dtype),
        compiler_params=pltpu.CompilerParams(
            dimension_semantics=("parallel", "parallel"), vmem_limit_bytes=VMEM_LIMIT),
        name=name,
    )(a1, a2, b)


def _cast_rows(rows, width):
    br = min(rows, max(8, (CAST_BLOCK_BYTES // (4 * width)) // 8 * 8))
    while rows % br:
        br -= 8
    return br


def _cast_kernel(w_ref, o_ref):
    o_ref[...] = w_ref[...].astype(o_ref.dtype)


def _split_cast_kernel(w_ref, main_ref, gate_ref, *, main_cols, gate_cols):
    off = 0
    for lo, hi in main_cols:
        main_ref[:, off:off + hi - lo] = w_ref[:, lo:hi].astype(BF16)
        off += hi - lo
    parts = [w_ref[:, lo:hi] for lo, hi in gate_cols]
    used = sum(hi - lo for lo, hi in gate_cols)
    parts.append(jnp.zeros((w_ref.shape[0], LANES - used), F32))
    gate_ref[...] = jnp.concatenate(parts, axis=1).astype(BF16)


def _cast_bf16(w, layer, name):
    _, r, c = w.shape
    br = _cast_rows(r, c)
    return pl.pallas_call(
        _cast_kernel,
        grid=(r // br,),
        in_specs=[pl.BlockSpec((None, br, c), lambda i: (layer, i, 0))],
        out_specs=pl.BlockSpec((br, c), lambda i: (i, 0)),
        out_shape=jax.ShapeDtypeStruct((r, c), BF16),
        compiler_params=pltpu.CompilerParams(dimension_semantics=("parallel",), vmem_limit_bytes=VMEM_LIMIT),
        name=name,
    )(w)


def _split_cast_bf16(w, layer, main_cols, gate_cols, name):
    _, r, c = w.shape
    br = _cast_rows(r, c)
    n_main = sum(hi - lo for lo, hi in main_cols)
    return pl.pallas_call(
        functools.partial(_split_cast_kernel, main_cols=main_cols, gate_cols=gate_cols),
        grid=(r // br,),
        in_specs=[pl.BlockSpec((None, br, c), lambda i: (layer, i, 0))],
        out_specs=[pl.BlockSpec((br, n_main), lambda i: (i, 0)), pl.BlockSpec((br, LANES), lambda i: (i, 0))],
        out_shape=[jax.ShapeDtypeStruct((r, n_main), BF16), jax.ShapeDtypeStruct((r, LANES), BF16)],
        compiler_params=pltpu.CompilerParams(dimension_semantics=("parallel",), vmem_limit_bytes=VMEM_LIMIT),
        name=name,
    )(w)


def _ln_kernel(x_ref, h_ref, g_ref, b_ref, o_ref, ob_ref):
    t = ALPHA * x_ref[...] + h_ref[...]
    mu = jnp.mean(t, axis=-1, keepdims=True)
    d = t - mu
    var = jnp.mean(d * d, axis=-1, keepdims=True)
    r = d * lax.rsqrt(var + LN_EPS) * g_ref[...] + b_ref[...]
    o_ref[...] = r
    ob_ref[...] = r.astype(BF16)


def _ln_residual(x, h, g, b, *, tm=256, name):
    m, d = x.shape
    tm = min(tm, m)
    row = pl.BlockSpec((tm, d), lambda i: (i, 0))
    vec = pl.BlockSpec((1, d), lambda i: (0, 0))
    return pl.pallas_call(
        _ln_kernel,
        grid=(m // tm,),
        in_specs=[row, row, vec, vec],
        out_specs=[row, row],
        out_shape=[jax.ShapeDtypeStruct((m, d), F32), jax.ShapeDtypeStruct((m, d), BF16)],
        compiler_params=pltpu.CompilerParams(
            dimension_semantics=("parallel",), vmem_limit_bytes=VMEM_LIMIT),
        name=name,
    )(x, h, g.reshape(1, d), b.reshape(1, d))


def _stack_masks():
    srow = lax.broadcasted_iota(jnp.int32, (ROWS, ROWS), 0)
    scol = lax.broadcasted_iota(jnp.int32, (ROWS, ROWS), 1)
    same_head = (srow // CHUNK) == (scol // CHUNK)
    causal = jnp.logical_and(same_head, srow >= scol)
    strict = jnp.logical_and(same_head, srow > scol)
    pick = (lax.broadcasted_iota(jnp.int32, (STACK, ROWS), 1) // CHUNK
            == lax.broadcasted_iota(jnp.int32, (STACK, ROWS), 0))
    return srow, scol, causal, strict, pick


def _stack(fn, heads):
    return jnp.concatenate([fn(h) for h in heads], axis=0)


def _stack_bcast(fn, heads):
    return jnp.concatenate([jnp.broadcast_to(fn(h), (CHUNK, fn(h).shape[1])) for h in heads], axis=0)


def _row_form(t_rows, pick):
    return jnp.sum(jnp.where(pick, jnp.concatenate([t_rows] * STACK, axis=1), 0.0), axis=0, keepdims=True)


def _head_rows(j):
    return slice(j * CHUNK, (j + 1) * CHUNK)


def _chunk_rows(ci):
    return slice(ci * CHUNK, (ci + 1) * CHUNK)


def _mixer_call(kernel, y_specs, extra_specs, scratch, args, b, t, tb, name):
    return pl.pallas_call(
        kernel,
        grid=(b, t // tb),
        in_specs=y_specs + extra_specs,
        out_specs=pl.BlockSpec((1, tb, GROUP_WIDTH), lambda i, c: (i, c, 0)),
        out_shape=jax.ShapeDtypeStruct((b, t, GROUP_WIDTH), BF16),
        scratch_shapes=scratch,
        compiler_params=pltpu.CompilerParams(
            dimension_semantics=("arbitrary", "arbitrary"), vmem_limit_bytes=VMEM_LIMIT),
        name=name,
    )(*args)


def _tok_spec(tb, width, blk):
    return pl.BlockSpec((1, tb, width), lambda i, c: (i, c, blk))


def _full_spec(shape):
    return pl.BlockSpec(shape, lambda i, c: (0,) * len(shape))


def _gdn_kernel(qkv_ref, z_ref, gate_ref, convw_ref, alog_ref, dtb_ref, normw_ref, o_ref,
                cbuf, cs_ref, s_ref, *, tb):
    hist = 8

    @pl.when(pl.program_id(1) == 0)
    def _():
        cbuf[0:hist, :] = jnp.zeros((hist, GDN_QKV), F32)
        s_ref[...] = jnp.zeros_like(s_ref)

    cbuf[hist:hist + tb, :] = qkv_ref[0]
    for off in range(0, GDN_QKV, LANES):
        acc = cbuf[hist:hist + tb, off:off + LANES] * convw_ref[CONV_K - 1:CONV_K, off:off + LANES]
        for s in range(1, CONV_K):
            acc = acc + (cbuf[hist - s:hist - s + tb, off:off + LANES]
                         * convw_ref[CONV_K - 1 - s:CONV_K - s, off:off + LANES])
        cs_ref[:, off:off + LANES] = acc * _sigmoid(acc)
    cbuf[0:hist, :] = cbuf[tb:tb + hist, :]

    groups = [range(g * STACK, (g + 1) * STACK) for g in range(GDN_HEADS // STACK)]

    row, col = _chunk_masks()
    srow, scol, causal, strict, pick = _stack_masks()

    def intra(ci):
        tok = _chunk_rows(ci)
        gates = gate_ref[0, tok, :]
        g_all = -jnp.exp(alog_ref[...]) * _softplus(gates + dtb_ref[...])
        beta_all = _sigmoid(gates)
        gam_all = _cumsum_rows(g_all, row, col)
        gam_t = gam_all.T

        qs, ks, vs, gam_cs, beta_cs, decays, a_mats = [], [], [], [], [], [], []
        for heads in groups:
            q = _stack(lambda h: cs_ref[tok, h * GDN_D:(h + 1) * GDN_D], heads)
            k = _stack(lambda h: cs_ref[tok, GROUP_WIDTH + h * GDN_D:GROUP_WIDTH + (h + 1) * GDN_D], heads)
            v = _stack(lambda h: cs_ref[tok, 2 * GROUP_WIDTH + h * GDN_D:2 * GROUP_WIDTH + (h + 1) * GDN_D], heads)
            q = q * (lax.rsqrt(jnp.sum(q * q, axis=-1, keepdims=True) + EPS) * GDN_D ** -0.5)
            k = k * lax.rsqrt(jnp.sum(k * k, axis=-1, keepdims=True) + EPS)
            gam_c = _stack(lambda h: gam_all[:, GA_OFF + h:GA_OFF + h + 1], heads)
            beta_c = _stack(lambda h: beta_all[:, GB_OFF + h:GB_OFF + h + 1], heads)
            gam_r = _row_form(gam_t[GA_OFF + heads[0]:GA_OFF + heads[0] + STACK, :], pick)
            decay = jnp.where(causal, jnp.exp(jnp.where(causal, gam_c - gam_r, 0.0)), 0.0)
            a_mats.append(jnp.where(strict, _dot_nt(k, k) * decay * beta_c, 0.0))
            qs.append(q); ks.append(k); vs.append(v)
            gam_cs.append(gam_c); beta_cs.append(beta_c); decays.append(decay)

        invs = _unit_lower_inverses(a_mats, srow, scol)
        out = []
        for g, heads in enumerate(groups):
            q, k, v, gam_c, beta_c = qs[g], ks[g], vs[g], gam_cs[g], beta_cs[g]
            egam = jnp.exp(gam_c)
            rhs = jnp.concatenate([v * beta_c, k * (beta_c * egam)], axis=1)
            sol = _dot(invs[g], rhs)
            qk = _dot_nt(q, k) * decays[g]
            gam_last = _stack_bcast(lambda h: gam_all[CHUNK - 1:CHUNK, GA_OFF + h:GA_OFF + h + 1], heads)
            k_end = k * jnp.exp(gam_last - gam_c)
            c_decs = [jnp.exp(gam_all[CHUNK - 1:CHUNK, GA_OFF + h:GA_OFF + h + 1]) for h in heads]
            out.append((sol, qk, q * egam, k_end, c_decs))
        return out

    def state_step(ci, pre):
        tok = _chunk_rows(ci)
        for g, heads in enumerate(groups):
            sol, qk, q_dec, k_end, c_decs = pre[g]
            v_new, q_state, states = [], [], []
            for j, h in enumerate(heads):
                sl = _head_rows(j)
                state = s_ref[h]
                both = _dot(jnp.concatenate([sol[sl, GDN_D:], q_dec[sl]], axis=0), state)
                v_new.append(sol[sl, :GDN_D] - both[:CHUNK])
                q_state.append(both[CHUNK:])
                states.append(state)
            o = jnp.concatenate(q_state, axis=0) + _dot(qk, jnp.concatenate(v_new, axis=0))
            for j, h in enumerate(heads):
                s_ref[h] = states[j] * c_decs[j] + _dot_tn(k_end[_head_rows(j)], v_new[j])
            o = o * lax.rsqrt(jnp.mean(o * o, axis=-1, keepdims=True) + EPS) * normw_ref[...]
            for j, h in enumerate(heads):
                zz = z_ref[0, tok, h * GDN_D:(h + 1) * GDN_D]
                o_ref[0, tok, h * GDN_D:(h + 1) * GDN_D] = (
                    o[_head_rows(j)] * (zz * _sigmoid(zz))).astype(o_ref.dtype)

    n_chunks = tb // CHUNK
    pre = intra(0)
    for ci in range(n_chunks):
        nxt = intra(ci + 1) if ci + 1 < n_chunks else None
        state_step(ci, pre)
        pre = nxt


def _gdn_mixer(y, yg, conv_w, alog_row, dtb_row, norm_w):
    b, t, _ = y.shape
    tb = min(GDN_BLOCK, t)
    return _mixer_call(
        functools.partial(_gdn_kernel, tb=tb),
        [_tok_spec(tb, GDN_QKV, 0), _tok_spec(tb, GROUP_WIDTH, GDN_QKV // GROUP_WIDTH), _tok_spec(tb, LANES, 0)],
        [_full_spec((CONV_K, GDN_QKV)), _full_spec((1, LANES)), _full_spec((1, LANES)), _full_spec((1, GDN_D))],
        [pltpu.VMEM((tb + 8, GDN_QKV), F32), pltpu.VMEM((tb, GDN_QKV), F32),
         pltpu.VMEM((GDN_HEADS, GDN_D, GDN_D), F32)],
        (y, y, yg, conv_w, alog_row, dtb_row, norm_w), b, t, tb, "gdn_mixer")


def _mlstm_kernel(q_ref, k_ref, v_ref, og_ref, gate_ref, gb_ref, normw_ref, o_ref, c_ref, n_ref, m_ref, *, tb):
    @pl.when(pl.program_id(1) == 0)
    def _():
        c_ref[...] = jnp.zeros_like(c_ref)
        n_ref[...] = jnp.zeros_like(n_ref)
        m_ref[...] = jnp.zeros_like(m_ref)

    groups = [range(g * STACK, (g + 1) * STACK) for g in range(ML_HEADS // STACK)]

    def chunk(ci, carry):
        tok = _chunk_rows(ci)
        row, col = _chunk_masks()
        _, _, causal, _, pick = _stack_masks()
        pre = gate_ref[0, tok, :] + gb_ref[...]
        fc_all = _cumsum_rows(_log_sigmoid(pre), row, col)
        fc_t = fc_all.T
        pre_t = pre.T

        st = []
        for heads in groups:
            h0 = heads[0]
            q = _stack(lambda h: q_ref[0, tok, h * ML_DK:(h + 1) * ML_DK], heads)
            k = _stack(lambda h: k_ref[0, tok, h * ML_DK:(h + 1) * ML_DK], heads) * ML_DK ** -0.5
            v = _stack(lambda h: v_ref[0, tok, h * ML_DV:(h + 1) * ML_DV], heads)
            fc_c = _stack(lambda h: fc_all[:, MF_OFF + h:MF_OFF + h + 1], heads)
            ic_c = _stack(lambda h: pre[:, MI_OFF + h:MI_OFF + h + 1], heads)
            fc_last = _stack_bcast(lambda h: fc_all[CHUNK - 1:CHUNK, MF_OFF + h:MF_OFF + h + 1], heads)
            fc_r = _row_form(fc_t[MF_OFF + h0:MF_OFF + h0 + STACK, :], pick)
            ic_r = _row_form(pre_t[MI_OFF + h0:MI_OFF + h0 + STACK, :], pick)
            d_log = jnp.where(causal, fc_c - fc_r + ic_r, -jnp.inf)
            m_intra = jnp.max(d_log, axis=-1, keepdims=True)
            qk = _dot_nt(q, k)
            src_end = fc_last - fc_c + ic_c
            m_src = jnp.concatenate(
                [jnp.broadcast_to(jnp.max(src_end[_head_rows(j)], axis=0, keepdims=True), (CHUNK, 1))
                 for j in range(STACK)], axis=0)
            m_st = _stack_bcast(lambda h: m_ref[h, 0:1, 0:1], heads)
            n_st = _stack_bcast(lambda h: n_ref[h, 0:1, :], heads)
            c_sts = [c_ref[h] for h in heads]
            m_inter = fc_c + m_st
            m_t = jnp.maximum(m_inter, m_intra)
            w_inter = jnp.exp(m_inter - m_t)
            w_intra = jnp.exp(d_log - m_t) * qk
            q_c = jnp.concatenate([_dot(q[_head_rows(j)], c_sts[j]) for j in range(STACK)], axis=0)
            num = w_inter * q_c + _dot(w_intra, v)
            den = (w_inter * jnp.sum(q * n_st, axis=-1, keepdims=True)
                   + jnp.sum(w_intra, axis=-1, keepdims=True))
            hh = num / jnp.maximum(jnp.abs(den), jnp.exp(-m_t))
            m_new = jnp.maximum(fc_last + m_st, m_src)
            w_state = jnp.exp(fc_last + m_st - m_new)
            kw = k * jnp.exp(src_end - m_new)
            st.append((heads, v, hh, m_new, w_state, kw, c_sts, n_st))

        for heads, v, hh, m_new, w_state, kw, c_sts, n_st in st:
            for j, h in enumerate(heads):
                sl = _head_rows(j)
                ws = w_state[j * CHUNK:j * CHUNK + 1, :]
                c_ref[h] = ws * c_sts[j] + _dot_tn(kw[sl], v[sl])
                n_ref[h] = jnp.broadcast_to(
                    ws * n_st[j * CHUNK:j * CHUNK + 1, :] + jnp.sum(kw[sl], axis=0, keepdims=True), (8, ML_DK))
                m_ref[h] = jnp.broadcast_to(m_new[j * CHUNK:j * CHUNK + 1, :], (8, LANES))
            hh = hh * lax.rsqrt(jnp.mean(hh * hh, axis=-1, keepdims=True) + EPS) * normw_ref[...]
            for j, h in enumerate(heads):
                gg = og_ref[0, tok, h * ML_DV:(h + 1) * ML_DV]
                o_ref[0, tok, h * ML_DV:(h + 1) * ML_DV] = (hh[_head_rows(j)] * _sigmoid(gg)).astype(o_ref.dtype)
        return carry

    for ci in range(tb // CHUNK):
        chunk(ci, None)


def _mlstm_mixer(y, yg, gb_row, norm_w):
    b, t, _ = y.shape
    tb = min(MIX_BLOCK, t)
    qk_w = ML_HEADS * ML_DK
    base = GDN_QKV + GROUP_WIDTH
    vblk = (base + 2 * qk_w) // GROUP_WIDTH
    return _mixer_call(
        functools.partial(_mlstm_kernel, tb=tb),
        [_tok_spec(tb, qk_w, base // qk_w), _tok_spec(tb, qk_w, base // qk_w + 1),
         _tok_spec(tb, GROUP_WIDTH, vblk), _tok_spec(tb, GROUP_WIDTH, vblk + 1), _tok_spec(tb, LANES, 0)],
        [_full_spec((1, LANES)), _full_spec((1, ML_DV))],
        [pltpu.VMEM((ML_HEADS, ML_DK, ML_DV), F32), pltpu.VMEM((ML_HEADS, 8, ML_DK), F32),
         pltpu.VMEM((ML_HEADS, 8, LANES), F32)],
        (y, y, y, y, yg, gb_row, norm_w), b, t, tb, "mlstm_mixer")


def _gla_group(q, k, v, bcum, bcum_last, end_decay_cols, states, causal):
    q_dec = q * jnp.exp(bcum)
    k_inv = k * jnp.exp(-bcum)
    attn = jnp.where(causal, _dot_nt(q_dec, k_inv), 0.0)
    o = _dot(attn, v) + jnp.concatenate(
        [_dot(q_dec[_head_rows(j)], states[j]) for j in range(STACK)], axis=0)
    k_end = k * jnp.exp(bcum_last - bcum)
    new_states = [states[j] * end_decay_cols[j] + _dot_tn(k_end[_head_rows(j)], v[_head_rows(j)])
                  for j in range(STACK)]
    return o, new_states


def _ret_kernel(q_ref, k_ref, v_ref, g_ref, cos_ref, sin_ref, o_ref, s_ref, *, tb):
    @pl.when(pl.program_id(1) == 0)
    def _():
        s_ref[...] = jnp.zeros_like(s_ref)

    groups = [range(g * STACK, (g + 1) * STACK) for g in range(RET_HEADS // STACK)]

    def chunk(ci, carry):
        tok = _chunk_rows(ci)
        _, _, causal, _, _ = _stack_masks()
        cos2 = jnp.concatenate([cos_ref[0, tok, :]] * STACK, axis=0)
        sin2 = jnp.concatenate([sin_ref[0, tok, :]] * STACK, axis=0)
        steps = (lax.broadcasted_iota(jnp.int32, (CHUNK, 1), 0) + 1).astype(F32)
        for heads in groups:
            log_gamma = [math.log(1.0 - 2.0 ** (-5.0 - h)) for h in heads]
            q = _stack(lambda h: q_ref[0, tok, h * RET_DK:(h + 1) * RET_DK], heads)
            k = _stack(lambda h: k_ref[0, tok, h * RET_DK:(h + 1) * RET_DK], heads)
            v = _stack(lambda h: v_ref[0, tok, h * RET_DV:(h + 1) * RET_DV], heads)
            q = q * cos2 + pltpu.roll(q, RET_DK // 2, 1) * sin2
            k = (k * cos2 + pltpu.roll(k, RET_DK // 2, 1) * sin2) * RET_DK ** -0.5
            bcum = jnp.concatenate([steps * lg for lg in log_gamma], axis=0)
            bcum_last = jnp.concatenate([jnp.full((CHUNK, 1), CHUNK * lg, F32) for lg in log_gamma], axis=0)
            ends = [jnp.exp(jnp.full((1, 1), CHUNK * lg, F32)) for lg in log_gamma]
            o, new_states = _gla_group(q, k, v, bcum, bcum_last, ends, [s_ref[h] for h in heads], causal)
            for j, h in enumerate(heads):
                s_ref[h] = new_states[j]
            o = o * lax.rsqrt(jnp.mean(o * o, axis=-1, keepdims=True) + EPS)
            for j, h in enumerate(heads):
                gg = g_ref[0, tok, h * RET_DV:(h + 1) * RET_DV]
                o_ref[0, tok, h * RET_DV:(h + 1) * RET_DV] = (
                    o[_head_rows(j)] * (gg * _sigmoid(gg))).astype(o_ref.dtype)
        return carry

    for ci in range(tb // CHUNK):
        chunk(ci, None)


def _ret_mixer(y, cos2, sin2):
    b, t, _ = y.shape
    tb = min(MIX_BLOCK, t)
    qk_w = RET_HEADS * RET_DK
    return _mixer_call(
        functools.partial(_ret_kernel, tb=tb),
        [_tok_spec(tb, qk_w, 0), _tok_spec(tb, qk_w, 1), _tok_spec(tb, GROUP_WIDTH, 1),
         _tok_spec(tb, GROUP_WIDTH, 2), _tok_spec(tb, RET_DK, 0), _tok_spec(tb, RET_DK, 0)],
        [],
        [pltpu.VMEM((RET_HEADS, RET_DK, RET_DV), F32)],
        (y, y, y, y, cos2, sin2), b, t, tb, "ret_mixer")


def _gla_kernel(q_ref, k_ref, v_ref, r_ref, la_ref, wup_ref, bup_ref, normw_ref, o_ref, s_ref, *, tb):
    @pl.when(pl.program_id(1) == 0)
    def _():
        s_ref[...] = jnp.zeros_like(s_ref)

    heads = range(GLA_HEADS)
    assert GLA_HEADS == STACK

    def chunk(ci, carry):
        tok = _chunk_rows(ci)
        row, col = _chunk_masks()
        _, _, causal, _, _ = _stack_masks()
        gate = _dot(la_ref[0, tok, :], wup_ref[...]) + bup_ref[...]
        bcum_all = _cumsum_rows(_log_sigmoid(gate) * (1.0 / GLA_TAU), row, col)
        q = _stack(lambda h: q_ref[0, tok, h * GLA_DK:(h + 1) * GLA_DK], heads) * GLA_DK ** -0.5
        k = _stack(lambda h: k_ref[0, tok, h * GLA_DK:(h + 1) * GLA_DK], heads)
        v = _stack(lambda h: v_ref[0, tok, h * GLA_DV:(h + 1) * GLA_DV], heads)
        bcum = _stack(lambda h: bcum_all[:, h * GLA_DK:(h + 1) * GLA_DK], heads)
        bcum_last = _stack_bcast(lambda h: bcum_all[CHUNK - 1:CHUNK, h * GLA_DK:(h + 1) * GLA_DK], heads)
        ends = [jnp.exp(bcum_all[:, h * GLA_DK:(h + 1) * GLA_DK].T[:, CHUNK - 1:CHUNK]) for h in heads]
        o, new_states = _gla_group(q, k, v, bcum, bcum_last, ends, [s_ref[h] for h in heads], causal)
        for h in heads:
            s_ref[h] = new_states[h]
        o = o * lax.rsqrt(jnp.mean(o * o, axis=-1, keepdims=True) + EPS) * normw_ref[...]
        for h in heads:
            rr = r_ref[0, tok, h * GLA_DV:(h + 1) * GLA_DV]
            o_ref[0, tok, h * GLA_DV:(h + 1) * GLA_DV] = (
                o[_head_rows(h)] * (rr * _sigmoid(rr))).astype(o_ref.dtype)
        return carry

    for ci in range(tb // CHUNK):
        chunk(ci, None)


def _gla_mixer(y, yla, w_up_pad, b_up, norm_w):
    b, t, _ = y.shape
    tb = min(MIX_BLOCK, t)
    qk_w = GLA_HEADS * GLA_DK
    base = 2 * RET_HEADS * RET_DK + 2 * GROUP_WIDTH
    vblk = (base + 2 * qk_w) // GROUP_WIDTH
    return _mixer_call(
        functools.partial(_gla_kernel, tb=tb),
        [_tok_spec(tb, qk_w, base // qk_w), _tok_spec(tb, qk_w, base // qk_w + 1),
         _tok_spec(tb, GROUP_WIDTH, vblk), _tok_spec(tb, GROUP_WIDTH, vblk + 1), _tok_spec(tb, LANES, 0)],
        [_full_spec((LANES, qk_w)), _full_spec((1, qk_w)), _full_spec((1, GLA_DV))],
        [pltpu.VMEM((GLA_HEADS, GLA_DK, GLA_DV), F32)],
        (y, y, y, y, yla, w_up_pad, b_up, norm_w), b, t, tb, "gla_mixer")


def _pad_lanes(v, offset=0):
    return jnp.zeros((1, LANES), F32).at[0, offset:offset + v.shape[0]].set(v.astype(F32))


def _even_mixer(xb, w_in_all, j, conv_w, a_log, dt_bias, gdn_norm_w, ml_gate_b, ml_norm_w, w_out_all):
    b, t, d = xb.shape
    x2 = xb.reshape(b * t, d)
    g_end = GDN_QKV + GROUP_WIDTH
    m_start = g_end + 2 * GDN_HEADS
    m_end = m_start + EVEN_MAIN - g_end
    w_main, w_gate = _split_cast_bf16(
        w_in_all, j, main_cols=((0, g_end), (m_start, m_end)),
        gate_cols=((g_end, m_start), (m_end, w_in_all.shape[2])), name="even_w_in_cast")
    w_out = _cast_bf16(w_out_all, j, "even_w_out_cast")
    y = _matmul(x2, w_main, out_dtype=F32, tm=1024, tn=1024, tk=d, name="even_in_proj").reshape(b, t, EVEN_MAIN)
    yg = _matmul(x2, w_gate, out_dtype=F32, tm=1024, tn=LANES, tk=d, name="even_gate_proj").reshape(b, t, LANES)
    o_a = _gdn_mixer(y, yg, conv_w.astype(F32), _pad_lanes(a_log, GA_OFF), _pad_lanes(dt_bias, GA_OFF),
                     gdn_norm_w.reshape(1, GDN_D).astype(F32))
    o_b = _mlstm_mixer(y, yg, _pad_lanes(ml_gate_b, MI_OFF), ml_norm_w.reshape(1, ML_DV).astype(F32))
    return _matmul_pair(o_a.reshape(b * t, GROUP_WIDTH), o_b.reshape(b * t, GROUP_WIDTH), w_out,
                        out_dtype=F32, tm=1024, tn=1024, name="even_out_proj")


def _odd_mixer(xb, positions, w_in_all, j, gla_w_up, gla_b_up, gla_norm_w, w_out_all):
    b, t, d = xb.shape
    x2 = xb.reshape(b * t, d)
    w_main, w_la = _split_cast_bf16(
        w_in_all, j, main_cols=((0, ODD_MAIN),), gate_cols=((ODD_MAIN, w_in_all.shape[2]),), name="odd_w_in_cast")
    w_out = _cast_bf16(w_out_all, j, "odd_w_out_cast")
    y = _matmul(x2, w_main, out_dtype=F32, tm=1024, tn=1024, tk=d, name="odd_in_proj").reshape(b, t, ODD_MAIN)
    yla = _matmul(x2, w_la, out_dtype=F32, tm=1024, tn=LANES, tk=d, name="odd_gate_proj").reshape(b, t, LANES)
    inv_freq = 1.0 / (ROPE_BASE ** jnp.linspace(0.0, 1.0, RET_DK // 2, dtype=F32))
    theta = positions.astype(F32)[:, :, None] * inv_freq
    cos, sin = jnp.cos(theta), jnp.sin(theta)
    cos2 = jnp.concatenate([cos, cos], axis=-1)
    sin2 = jnp.concatenate([-sin, sin], axis=-1)
    o_c = _ret_mixer(y, cos2, sin2)
    w_up_pad = jnp.pad(gla_w_up.astype(F32), ((0, LANES - GLA_RANK), (0, 0)))
    o_d = _gla_mixer(y, yla, w_up_pad, gla_b_up.reshape(1, -1).astype(F32),
                     gla_norm_w.reshape(1, GLA_DV).astype(F32))
    return _matmul_pair(o_c.reshape(b * t, GROUP_WIDTH), o_d.reshape(b * t, GROUP_WIDTH), w_out,
                        out_dtype=F32, tm=1024, tn=1024, name="odd_out_proj")


def _mlp(xb2, w_up_all, w_down_all, layer):
    w_up = _cast_bf16(w_up_all, layer, f"mlp_w_up_cast_{layer}")
    w_down = _cast_bf16(w_down_all, layer, f"mlp_w_down_cast_{layer}")
    hdn = _matmul(xb2, w_up, out_dtype=BF16, tm=1024, tn=1024, tk=D_MODEL, relu2=True, name=f"mlp_up_{layer}")
    return _matmul(hdn, w_down, out_dtype=F32, tm=1024, tn=1024, tk=2048, name=f"mlp_down_{layer}")


def kernel(x, positions, e_w_in, e_conv_w, e_a_log, e_dt_bias, e_gdn_norm_w, e_mlstm_gate_b, e_mlstm_norm_w, e_w_out, o_w_in, o_gla_w_up, o_gla_b_up, o_gla_norm_w, o_w_out, ln_mix_g, ln_mix_b, mlp_w_up, mlp_w_down, ln_mlp_g, ln_mlp_b):
    b, t, d = x.shape
    x2 = x.reshape(b * t, d)
    xb2 = x2.astype(BF16)
    for layer in range(DEPTH):
        j = layer // 2
        xb = xb2.reshape(b, t, d)
        if layer % 2 == 0:
            h = _even_mixer(xb, e_w_in, j, e_conv_w[j], e_a_log[j], e_dt_bias[j], e_gdn_norm_w[j],
                            e_mlstm_gate_b[j], e_mlstm_norm_w[j], e_w_out)
        else:
            h = _odd_mixer(xb, positions, o_w_in, j, o_gla_w_up[j], o_gla_b_up[j], o_gla_norm_w[j], o_w_out)
        x2, xb2 = _ln_residual(x2, h, ln_mix_g[layer], ln_mix_b[layer], name=f"ln_mix_{layer}")
        f = _mlp(xb2, mlp_w_up, mlp_w_down, layer)
        x2, xb2 = _ln_residual(x2, f, ln_mlp_g[layer], ln_mlp_b[layer], name=f"ln_mlp_{layer}")
    return x2.reshape(b, t, d)
```

```python
import functools
import math

import jax
import jax.numpy as jnp
from jax import lax
from jax.experimental import pallas as pl
from jax.experimental.pallas import tpu as pltpu

F32 = jnp.float32
BF16 = jnp.bfloat16
HIGHEST = lax.Precision.HIGHEST

D_MODEL = 4096
DEPTH = 2
CHUNK = 64
D_FF = 4 * D_MODEL
GROUP_WIDTH = D_MODEL // 2
ALPHA = (2.0 * DEPTH) ** 0.25
EPS = 1e-6
LN_EPS = 1e-5

GDN_HEADS = 16
GDN_D = GROUP_WIDTH // GDN_HEADS
CONV_K = 4
GDN_QKV = 3 * GROUP_WIDTH
ML_HEADS = 8
ML_DV = GROUP_WIDTH // ML_HEADS
ML_DK = ML_DV // 2
RET_HEADS = 8
RET_DV = GROUP_WIDTH // RET_HEADS
RET_DK = RET_DV // 2
ROPE_BASE = 10000.0
GLA_HEADS = 4
GLA_DV = GROUP_WIDTH // GLA_HEADS
GLA_DK = GLA_DV // 2
GLA_RANK = 16
GLA_TAU = 16.0

GDN_STACK, ML_STACK, RET_STACK, GLA_STACK = 2, 4, 4, 4
GDN_BLOCK = 2 * CHUNK
MIX_BLOCK = 4 * CHUNK
LANES = 128
EVEN_MAIN = GDN_QKV + GROUP_WIDTH + 2 * ML_HEADS * ML_DK + 2 * GROUP_WIDTH
ODD_MAIN = 2 * RET_HEADS * RET_DK + 2 * GROUP_WIDTH + 2 * GLA_HEADS * GLA_DK + 2 * GROUP_WIDTH
GA_OFF, GB_OFF, MI_OFF, MF_OFF = 0, GDN_HEADS, 2 * GDN_HEADS, 2 * GDN_HEADS + ML_HEADS

BRANCH_DTYPE = BF16
VMEM_LIMIT = 56 * 1024 * 1024
CAST_BLOCK_BYTES = 8 * 1024 * 1024


def _sigmoid(x):
    return 1.0 / (1.0 + jnp.exp(-x))


def _softplus(x):
    return jnp.maximum(x, 0.0) + jnp.log1p(jnp.exp(-jnp.abs(x)))


def _log_sigmoid(x):
    return -_softplus(-x)


def _mxu(x):
    return x.astype(BF16)


def _dot(a, b):
    return jnp.dot(_mxu(a), _mxu(b), preferred_element_type=F32)


def _dot_nt(a, b):
    return lax.dot_general(_mxu(a), _mxu(b), (((1,), (1,)), ((), ())), preferred_element_type=F32)


def _dot_tn(a, b):
    return _dot(a.T, b)


def _chunk_masks():
    row = lax.broadcasted_iota(jnp.int32, (CHUNK, CHUNK), 0)
    col = lax.broadcasted_iota(jnp.int32, (CHUNK, CHUNK), 1)
    return row, col


def _cumsum_rows(x, row, col):
    tril = jnp.where(row >= col, 1.0, 0.0).astype(F32)
    return jnp.dot(tril, x, precision=HIGHEST, preferred_element_type=F32)


def _unit_lower_inverses(mats, row, col):
    eye = jnp.where(row == col, 1.0, 0.0).astype(F32)
    same = (row >> 1) == (col >> 1)
    invs = [eye - jnp.where(same, a, 0.0) for a in mats]
    shift = 2
    while (1 << (shift - 1)) < CHUNK:
        same2 = (row >> shift) == (col >> shift)
        off_mask = jnp.logical_and(same2, jnp.logical_not(same))
        inv_b = [_mxu(inv) for inv in invs]
        tmp = [_dot(jnp.where(off_mask, a, 0.0), ib) for a, ib in zip(mats, inv_b)]
        invs = [inv - _dot(ib, t) for inv, ib, t in zip(invs, inv_b, tmp)]
        same = same2
        shift += 1
    return invs


def _mm_kernel_single(a_ref, b_ref, o_ref, *, relu2):
    r = _dot(a_ref[...], b_ref[...])
    if relu2:
        r = jnp.square(jnp.maximum(r, 0.0))
    o_ref[...] = r.astype(o_ref.dtype)


def _mm_kernel_acc(a_ref, b_ref, o_ref, acc_ref, *, nk, relu2):
    k = pl.program_id(2)

    @pl.when(k == 0)
    def _():
        acc_ref[...] = jnp.zeros_like(acc_ref)

    acc_ref[...] += _dot(a_ref[...], b_ref[...])

    @pl.when(k == nk - 1)
    def _():
        r = acc_ref[...]
        if relu2:
            r = jnp.square(jnp.maximum(r, 0.0))
        o_ref[...] = r.astype(o_ref.dtype)


def _matmul(a, b, *, out_dtype, tm, tn, tk, relu2=False, name):
    m, k = a.shape
    _, n = b.shape
    tm, tn, tk = min(tm, m), min(tn, n), min(tk, k)
    assert m % tm == 0 and n % tn == 0 and k % tk == 0
    nk = k // tk
    if nk == 1:
        return pl.pallas_call(
            functools.partial(_mm_kernel_single, relu2=relu2),
            grid=(m // tm, n // tn),
            in_specs=[pl.BlockSpec((tm, k), lambda i, j: (i, 0)),
                      pl.BlockSpec((k, tn), lambda i, j: (0, j))],
            out_specs=pl.BlockSpec((tm, tn), lambda i, j: (i, j)),
            out_shape=jax.ShapeDtypeStruct((m, n), out_dtype),
            compiler_params=pltpu.CompilerParams(
                dimension_semantics=("parallel", "parallel"), vmem_limit_bytes=VMEM_LIMIT),
            name=name,
        )(a, b)
    return pl.pallas_call(
        functools.partial(_mm_kernel_acc, nk=nk, relu2=relu2),
        grid=(m // tm, n // tn, nk),
        in_specs=[pl.BlockSpec((tm, tk), lambda i, j, kk: (i, kk)),
                  pl.BlockSpec((tk, tn), lambda i, j, kk: (kk, j))],
        out_specs=pl.BlockSpec((tm, tn), lambda i, j, kk: (i, j)),
        out_shape=jax.ShapeDtypeStruct((m, n), out_dtype),
        scratch_shapes=[pltpu.VMEM((tm, tn), F32)],
        compiler_params=pltpu.CompilerParams(
            dimension_semantics=("parallel", "parallel", "arbitrary"), vmem_limit_bytes=VMEM_LIMIT),
        name=name,
    )(a, b)


def _mm_pair_kernel(a1_ref, a2_ref, b_ref, o_ref):
    k1 = a1_ref.shape[1]
    r = _dot(a1_ref[...], b_ref[0:k1, :]) + _dot(a2_ref[...], b_ref[k1:, :])
    o_ref[...] = r.astype(o_ref.dtype)


def _matmul_pair(a1, a2, b, *, out_dtype, tm, tn, name):
    m, k1 = a1.shape
    k2 = a2.shape[1]
    n = b.shape[1]
    tm, tn = min(tm, m), min(tn, n)
    assert m % tm == 0 and n % tn == 0 and b.shape[0] == k1 + k2
    return pl.pallas_call(
        _mm_pair_kernel,
        grid=(m // tm, n // tn),
        in_specs=[pl.BlockSpec((tm, k1), lambda i, j: (i, 0)),
                  pl.BlockSpec((tm, k2), lambda i, j: (i, 0)),
                  pl.BlockSpec((k1 + k2, tn), lambda i, j: (0, j))],
        out_specs=pl.BlockSpec((tm, tn), lambda i, j: (i, j)),
        out_shape=jax.ShapeDtypeStruct((m, n), out_dtype),
        compiler_params=pltpu.CompilerParams(
            dimension_semantics=("parallel", "parallel"), vmem_limit_bytes=VMEM_LIMIT),
        name=name,
    )(a1, a2, b)


def _cast_rows(rows, width):
    br = min(rows, max(8, (CAST_BLOCK_BYTES // (4 * width)) // 8 * 8))
    while rows % br:
        br -= 8
    return br


def _cast_kernel(w_ref, o_ref):
    o_ref[...] = w_ref[...].astype(o_ref.dtype)


def _cast_bf16(w, layer, name):
    _, r, c = w.shape
    br = _cast_rows(r, c)
    return pl.pallas_call(
        _cast_kernel,
        grid=(r // br,),
        in_specs=[pl.BlockSpec((None, br, c), lambda i: (layer, i, 0))],
        out_specs=pl.BlockSpec((br, c), lambda i: (i, 0)),
        out_shape=jax.ShapeDtypeStruct((r, c), BF16),
        compiler_params=pltpu.CompilerParams(dimension_semantics=("parallel",), vmem_limit_bytes=VMEM_LIMIT),
        name=name,
    )(w)


def _ln_kernel(x_ref, h_ref, g_ref, b_ref, o_ref, ob_ref):
    t = ALPHA * x_ref[...] + h_ref[...].astype(F32)
    mu = jnp.mean(t, axis=-1, keepdims=True)
    d = t - mu
    var = jnp.mean(d * d, axis=-1, keepdims=True)
    r = d * lax.rsqrt(var + LN_EPS) * g_ref[...] + b_ref[...]
    o_ref[...] = r
    ob_ref[...] = r.astype(BF16)


def _ln_residual(x, h, g, b, *, tm=256, name):
    m, d = x.shape
    tm = min(tm, m)
    row = pl.BlockSpec((tm, d), lambda i: (i, 0))
    vec = pl.BlockSpec((1, d), lambda i: (0, 0))
    return pl.pallas_call(
        _ln_kernel,
        grid=(m // tm,),
        in_specs=[row, row, vec, vec],
        out_specs=[row, row],
        out_shape=[jax.ShapeDtypeStruct((m, d), F32), jax.ShapeDtypeStruct((m, d), BF16)],
        compiler_params=pltpu.CompilerParams(
            dimension_semantics=("parallel",), vmem_limit_bytes=VMEM_LIMIT),
        name=name,
    )(x, h, g.reshape(1, d), b.reshape(1, d))


def _stack_masks(stack):
    rows = stack * CHUNK
    srow = lax.broadcasted_iota(jnp.int32, (rows, rows), 0)
    scol = lax.broadcasted_iota(jnp.int32, (rows, rows), 1)
    same_head = (srow // CHUNK) == (scol // CHUNK)
    causal = jnp.logical_and(same_head, srow >= scol)
    strict = jnp.logical_and(same_head, srow > scol)
    pick = (lax.broadcasted_iota(jnp.int32, (stack, rows), 1) // CHUNK
            == lax.broadcasted_iota(jnp.int32, (stack, rows), 0))
    return srow, scol, causal, strict, pick


def _head_groups(n_heads, stack):
    return [range(g * stack, (g + 1) * stack) for g in range(n_heads // stack)]


def _stack(fn, heads):
    return jnp.concatenate([fn(h) for h in heads], axis=0)


def _stack_bcast(fn, heads):
    return jnp.concatenate([jnp.broadcast_to(fn(h), (CHUNK, fn(h).shape[1])) for h in heads], axis=0)


def _row_form(t_rows, pick):
    return jnp.sum(jnp.where(pick, jnp.concatenate([t_rows] * t_rows.shape[0], axis=1), 0.0),
                   axis=0, keepdims=True)


def _head_rows(j):
    return slice(j * CHUNK, (j + 1) * CHUNK)


def _chunk_rows(ci):
    return slice(ci * CHUNK, (ci + 1) * CHUNK)


def _mixer_call(kernel, y_specs, extra_specs, scratch, args, b, t, tb, name):
    return pl.pallas_call(
        kernel,
        grid=(b, t // tb),
        in_specs=y_specs + extra_specs,
        out_specs=pl.BlockSpec((1, tb, GROUP_WIDTH), lambda i, c: (i, c, 0)),
        out_shape=jax.ShapeDtypeStruct((b, t, GROUP_WIDTH), BF16),
        scratch_shapes=scratch,
        compiler_params=pltpu.CompilerParams(
            dimension_semantics=("arbitrary", "arbitrary"), vmem_limit_bytes=VMEM_LIMIT),
        name=name,
    )(*args)


def _tok_spec(tb, width, blk):
    return pl.BlockSpec((1, tb, width), lambda i, c: (i, c, blk))


def _full_spec(shape):
    return pl.BlockSpec(shape, lambda i, c: (0,) * len(shape))


def _gdn_kernel(qkv_ref, z_ref, gate_ref, convw_ref, alog_ref, dtb_ref, normw_ref, o_ref,
                cbuf, cs_ref, s_ref, *, tb):
    hist = 8

    @pl.when(pl.program_id(1) == 0)
    def _():
        cbuf[0:hist, :] = jnp.zeros((hist, GDN_QKV), F32)
        s_ref[...] = jnp.zeros_like(s_ref)

    cbuf[hist:hist + tb, :] = qkv_ref[0]
    for off in range(0, GDN_QKV, LANES):
        acc = cbuf[hist:hist + tb, off:off + LANES] * convw_ref[CONV_K - 1:CONV_K, off:off + LANES]
        for s in range(1, CONV_K):
            acc = acc + (cbuf[hist - s:hist - s + tb, off:off + LANES]
                         * convw_ref[CONV_K - 1 - s:CONV_K - s, off:off + LANES])
        cs_ref[:, off:off + LANES] = acc * _sigmoid(acc)
    cbuf[0:hist, :] = cbuf[tb:tb + hist, :]

    groups = _head_groups(GDN_HEADS, GDN_STACK)

    row, col = _chunk_masks()
    srow, scol, causal, strict, pick = _stack_masks(GDN_STACK)

    def intra(ci):
        tok = _chunk_rows(ci)
        gates = gate_ref[0, tok, :]
        g_all = -jnp.exp(alog_ref[...]) * _softplus(gates + dtb_ref[...])
        beta_all = _sigmoid(gates)
        gam_all = _cumsum_rows(g_all, row, col)
        gam_t = gam_all.T

        qs, ks, vs, gam_cs, beta_cs, decays, a_mats = [], [], [], [], [], [], []
        for heads in groups:
            q = _stack(lambda h: cs_ref[tok, h * GDN_D:(h + 1) * GDN_D], heads)
            k = _stack(lambda h: cs_ref[tok, GROUP_WIDTH + h * GDN_D:GROUP_WIDTH + (h + 1) * GDN_D], heads)
            v = _stack(lambda h: cs_ref[tok, 2 * GROUP_WIDTH + h * GDN_D:2 * GROUP_WIDTH + (h + 1) * GDN_D], heads)
            q = q * (lax.rsqrt(jnp.sum(q * q, axis=-1, keepdims=True) + EPS) * GDN_D ** -0.5)
            k = k * lax.rsqrt(jnp.sum(k * k, axis=-1, keepdims=True) + EPS)
            gam_c = _stack(lambda h: gam_all[:, GA_OFF + h:GA_OFF + h + 1], heads)
            beta_c = _stack(lambda h: beta_all[:, GB_OFF + h:GB_OFF + h + 1], heads)
            gam_r = _row_form(gam_t[GA_OFF + heads[0]:GA_OFF + heads[0] + GDN_STACK, :], pick)
            decay = jnp.where(causal, jnp.exp(jnp.where(causal, gam_c - gam_r, 0.0)), 0.0)
            a_mats.append(jnp.where(strict, _dot_nt(k, k) * decay * beta_c, 0.0))
            qs.append(q); ks.append(k); vs.append(v)
            gam_cs.append(gam_c); beta_cs.append(beta_c); decays.append(decay)

        invs = _unit_lower_inverses(a_mats, srow, scol)
        out = []
        for g, heads in enumerate(groups):
            q, k, v, gam_c, beta_c = qs[g], ks[g], vs[g], gam_cs[g], beta_cs[g]
            egam = jnp.exp(gam_c)
            rhs = jnp.concatenate([v * beta_c, k * (beta_c * egam)], axis=1)
            sol = _dot(invs[g], rhs)
            qk = _dot_nt(q, k) * decays[g]
            gam_last = _stack_bcast(lambda h: gam_all[CHUNK - 1:CHUNK, GA_OFF + h:GA_OFF + h + 1], heads)
            k_end = k * jnp.exp(gam_last - gam_c)
            c_decs = [jnp.exp(gam_all[CHUNK - 1:CHUNK, GA_OFF + h:GA_OFF + h + 1]) for h in heads]
            out.append((sol, qk, q * egam, k_end, c_decs))
        return out

    def state_step(ci, pre):
        tok = _chunk_rows(ci)
        for g, heads in enumerate(groups):
            sol, qk, q_dec, k_end, c_decs = pre[g]
            v_new, q_state, states = [], [], []
            for j, h in enumerate(heads):
                sl = _head_rows(j)
                state = s_ref[h]
                both = _dot(jnp.concatenate([sol[sl, GDN_D:], q_dec[sl]], axis=0), state)
                v_new.append(sol[sl, :GDN_D] - both[:CHUNK])
                q_state.append(both[CHUNK:])
                states.append(state)
            o = jnp.concatenate(q_state, axis=0) + _dot(qk, jnp.concatenate(v_new, axis=0))
            for j, h in enumerate(heads):
                s_ref[h] = states[j] * c_decs[j] + _dot_tn(k_end[_head_rows(j)], v_new[j])
            o = o * lax.rsqrt(jnp.mean(o * o, axis=-1, keepdims=True) + EPS) * normw_ref[...]
            for j, h in enumerate(heads):
                zz = z_ref[0, tok, h * GDN_D:(h + 1) * GDN_D]
                o_ref[0, tok, h * GDN_D:(h + 1) * GDN_D] = (
                    o[_head_rows(j)] * (zz * _sigmoid(zz))).astype(o_ref.dtype)

    n_chunks = tb // CHUNK
    pre = intra(0)
    for ci in range(n_chunks):
        nxt = intra(ci + 1) if ci + 1 < n_chunks else None
        state_step(ci, pre)
        pre = nxt


def _gdn_mixer(y, yg, conv_w, alog_row, dtb_row, norm_w):
    b, t, _ = y.shape
    tb = min(GDN_BLOCK, t)
    return _mixer_call(
        functools.partial(_gdn_kernel, tb=tb),
        [_tok_spec(tb, GDN_QKV, 0), _tok_spec(tb, GROUP_WIDTH, GDN_QKV // GROUP_WIDTH), _tok_spec(tb, LANES, 0)],
        [_full_spec((CONV_K, GDN_QKV)), _full_spec((1, LANES)), _full_spec((1, LANES)), _full_spec((1, GDN_D))],
        [pltpu.VMEM((tb + 8, GDN_QKV), F32), pltpu.VMEM((tb, GDN_QKV), F32),
         pltpu.VMEM((GDN_HEADS, GDN_D, GDN_D), F32)],
        (y, y, yg, conv_w, alog_row, dtb_row, norm_w), b, t, tb, "gdn_mixer")


def _mlstm_kernel(q_ref, k_ref, v_ref, og_ref, gate_ref, gb_ref, normw_ref, o_ref, c_ref, n_ref, m_ref, *, tb):
    @pl.when(pl.program_id(1) == 0)
    def _():
        c_ref[...] = jnp.zeros_like(c_ref)
        n_ref[...] = jnp.zeros_like(n_ref)
        m_ref[...] = jnp.zeros_like(m_ref)

    groups = _head_groups(ML_HEADS, ML_STACK)
    n_chunks = tb // CHUNK
    row, col = _chunk_masks()
    _, _, causal, _, pick = _stack_masks(ML_STACK)

    m_sts = [_stack_bcast(lambda h: m_ref[h, 0:1, 0:1], heads) for heads in groups]

    def weights(ci):
        tok = _chunk_rows(ci)
        pre = gate_ref[0, tok, :] + gb_ref[...]
        fc_all = _cumsum_rows(_log_sigmoid(pre), row, col)
        fc_t = fc_all.T
        pre_t = pre.T
        per_group = []
        for g, heads in enumerate(groups):
            h0 = heads[0]
            q = _stack(lambda h: q_ref[0, tok, h * ML_DK:(h + 1) * ML_DK], heads)
            k = _stack(lambda h: k_ref[0, tok, h * ML_DK:(h + 1) * ML_DK], heads) * ML_DK ** -0.5
            v = _stack(lambda h: v_ref[0, tok, h * ML_DV:(h + 1) * ML_DV], heads)
            fc_c = _stack(lambda h: fc_all[:, MF_OFF + h:MF_OFF + h + 1], heads)
            ic_c = _stack(lambda h: pre[:, MI_OFF + h:MI_OFF + h + 1], heads)
            fc_last = _stack_bcast(lambda h: fc_all[CHUNK - 1:CHUNK, MF_OFF + h:MF_OFF + h + 1], heads)
            fc_r = _row_form(fc_t[MF_OFF + h0:MF_OFF + h0 + ML_STACK, :], pick)
            ic_r = _row_form(pre_t[MI_OFF + h0:MI_OFF + h0 + ML_STACK, :], pick)
            d_log = jnp.where(causal, fc_c - fc_r + ic_r, -jnp.inf)
            m_intra = jnp.max(d_log, axis=-1, keepdims=True)
            qk = _dot_nt(q, k)
            src_end = fc_last - fc_c + ic_c
            m_src = jnp.concatenate(
                [jnp.broadcast_to(jnp.max(src_end[_head_rows(j)], axis=0, keepdims=True), (CHUNK, 1))
                 for j in range(ML_STACK)], axis=0)
            m_st = m_sts[g]
            m_inter = fc_c + m_st
            m_t = jnp.maximum(m_inter, m_intra)
            w_inter = jnp.exp(m_inter - m_t)
            w_intra = jnp.exp(d_log - m_t) * qk
            m_new = jnp.maximum(fc_last + m_st, m_src)
            w_state = jnp.exp(fc_last + m_st - m_new)
            kw = k * jnp.exp(src_end - m_new)
            m_sts[g] = m_new
            per_group.append((q, w_inter, _dot(w_intra, v), jnp.sum(w_intra, axis=-1, keepdims=True),
                              jnp.exp(-m_t), w_state,
                              [_dot_tn(kw[_head_rows(j)], v[_head_rows(j)]) for j in range(ML_STACK)],
                              [jnp.sum(kw[_head_rows(j)], axis=0, keepdims=True) for j in range(ML_STACK)]))
        return per_group

    c_sts = [[c_ref[h] for h in heads] for heads in groups]
    n_sts = [[n_ref[h, 0:1, :] for h in heads] for heads in groups]
    cur = weights(0)
    for ci in range(n_chunks):
        tok = _chunk_rows(ci)
        nxt = weights(ci + 1) if ci + 1 < n_chunks else None
        for g, heads in enumerate(groups):
            q, w_inter, intra_num, intra_den, floor, w_state, kv, ksum = cur[g]
            q_c = jnp.concatenate([_dot(q[_head_rows(j)], c_sts[g][j]) for j in range(ML_STACK)], axis=0)
            n_rows = jnp.concatenate(
                [jnp.broadcast_to(n_sts[g][j], (CHUNK, ML_DK)) for j in range(ML_STACK)], axis=0)
            num = w_inter * q_c + intra_num
            den = w_inter * jnp.sum(q * n_rows, axis=-1, keepdims=True) + intra_den
            hh = num / jnp.maximum(jnp.abs(den), floor)
            hh = hh * lax.rsqrt(jnp.mean(hh * hh, axis=-1, keepdims=True) + EPS) * normw_ref[...]
            for j, h in enumerate(heads):
                gg = og_ref[0, tok, h * ML_DV:(h + 1) * ML_DV]
                o_ref[0, tok, h * ML_DV:(h + 1) * ML_DV] = (hh[_head_rows(j)] * _sigmoid(gg)).astype(o_ref.dtype)
                ws = w_state[j * CHUNK:j * CHUNK + 1, :]
                c_sts[g][j] = ws * c_sts[g][j] + kv[j]
                n_sts[g][j] = ws * n_sts[g][j] + ksum[j]
        cur = nxt
    for g, heads in enumerate(groups):
        for j, h in enumerate(heads):
            c_ref[h] = c_sts[g][j]
            n_ref[h] = jnp.broadcast_to(n_sts[g][j], (8, ML_DK))
            m_ref[h] = jnp.broadcast_to(m_sts[g][j * CHUNK:j * CHUNK + 1, :], (8, LANES))


def _mlstm_mixer(y, yg, gb_row, norm_w):
    b, t, _ = y.shape
    tb = min(MIX_BLOCK, t)
    qk_w = ML_HEADS * ML_DK
    base = GDN_QKV + GROUP_WIDTH
    vblk = (base + 2 * qk_w) // GROUP_WIDTH
    return _mixer_call(
        functools.partial(_mlstm_kernel, tb=tb),
        [_tok_spec(tb, qk_w, base // qk_w), _tok_spec(tb, qk_w, base // qk_w + 1),
         _tok_spec(tb, GROUP_WIDTH, vblk), _tok_spec(tb, GROUP_WIDTH, vblk + 1), _tok_spec(tb, LANES, 0)],
        [_full_spec((1, LANES)), _full_spec((1, ML_DV))],
        [pltpu.VMEM((ML_HEADS, ML_DK, ML_DV), F32), pltpu.VMEM((ML_HEADS, 8, ML_DK), F32),
         pltpu.VMEM((ML_HEADS, 8, LANES), F32)],
        (y, y, y, y, yg, gb_row, norm_w), b, t, tb, "mlstm_mixer")


def _gla_intra(q, k, v, bcum, bcum_last, stack, causal):
    q_dec = q * jnp.exp(bcum)
    k_inv = k * jnp.exp(-bcum)
    attn = jnp.where(causal, _dot_nt(q_dec, k_inv), 0.0)
    o_intra = _dot(attn, v)
    k_end = k * jnp.exp(bcum_last - bcum)
    kv = [_dot_tn(k_end[_head_rows(j)], v[_head_rows(j)]) for j in range(stack)]
    return q_dec, o_intra, kv


def _gla_recur(q_dec, o_intra, kv, end_decay_cols, states):
    o = o_intra + jnp.concatenate(
        [_dot(q_dec[_head_rows(j)], states[j]) for j in range(len(states))], axis=0)
    return o, [states[j] * end_decay_cols[j] + kv[j] for j in range(len(states))]


def _ret_kernel(q_ref, k_ref, v_ref, g_ref, cos_ref, sin_ref, o_ref, s_ref, *, tb):
    @pl.when(pl.program_id(1) == 0)
    def _():
        s_ref[...] = jnp.zeros_like(s_ref)

    groups = _head_groups(RET_HEADS, RET_STACK)

    _, _, causal, _, _ = _stack_masks(RET_STACK)
    steps = (lax.broadcasted_iota(jnp.int32, (CHUNK, 1), 0) + 1).astype(F32)

    def intra(ci, heads):
        tok = _chunk_rows(ci)
        cos2 = jnp.concatenate([cos_ref[0, tok, :]] * RET_STACK, axis=0)
        sin2 = jnp.concatenate([sin_ref[0, tok, :]] * RET_STACK, axis=0)
        log_gamma = [math.log(1.0 - 2.0 ** (-5.0 - h)) for h in heads]
        q = _stack(lambda h: q_ref[0, tok, h * RET_DK:(h + 1) * RET_DK], heads)
        k = _stack(lambda h: k_ref[0, tok, h * RET_DK:(h + 1) * RET_DK], heads)
        v = _stack(lambda h: v_ref[0, tok, h * RET_DV:(h + 1) * RET_DV], heads)
        q = q * cos2 + pltpu.roll(q, RET_DK // 2, 1) * sin2
        k = (k * cos2 + pltpu.roll(k, RET_DK // 2, 1) * sin2) * RET_DK ** -0.5
        bcum = jnp.concatenate([steps * lg for lg in log_gamma], axis=0)
        bcum_last = jnp.concatenate([jnp.full((CHUNK, 1), CHUNK * lg, F32) for lg in log_gamma], axis=0)
        return _gla_intra(q, k, v, bcum, bcum_last, RET_STACK, causal)

    n_chunks = tb // CHUNK
    pre = [[intra(ci, heads) for heads in groups] for ci in range(n_chunks)]
    for g, heads in enumerate(groups):
        ends = [jnp.exp(jnp.full((1, 1), CHUNK * math.log(1.0 - 2.0 ** (-5.0 - h)), F32)) for h in heads]
        states = [s_ref[h] for h in heads]
        for ci in range(n_chunks):
            tok = _chunk_rows(ci)
            o, states = _gla_recur(*pre[ci][g], ends, states)
            o = o * lax.rsqrt(jnp.mean(o * o, axis=-1, keepdims=True) + EPS)
            for j, h in enumerate(heads):
                gg = g_ref[0, tok, h * RET_DV:(h + 1) * RET_DV]
                o_ref[0, tok, h * RET_DV:(h + 1) * RET_DV] = (
                    o[_head_rows(j)] * (gg * _sigmoid(gg))).astype(o_ref.dtype)
        for j, h in enumerate(heads):
            s_ref[h] = states[j]


def _ret_mixer(y, cos2, sin2):
    b, t, _ = y.shape
    tb = min(MIX_BLOCK, t)
    qk_w = RET_HEADS * RET_DK
    return _mixer_call(
        functools.partial(_ret_kernel, tb=tb),
        [_tok_spec(tb, qk_w, 0), _tok_spec(tb, qk_w, 1), _tok_spec(tb, GROUP_WIDTH, 1),
         _tok_spec(tb, GROUP_WIDTH, 2), _tok_spec(tb, RET_DK, 0), _tok_spec(tb, RET_DK, 0)],
        [],
        [pltpu.VMEM((RET_HEADS, RET_DK, RET_DV), F32)],
        (y, y, y, y, cos2, sin2), b, t, tb, "ret_mixer")


def _gla_kernel(q_ref, k_ref, v_ref, r_ref, la_ref, wup_ref, bup_ref, normw_ref, o_ref, s_ref, *, tb):
    @pl.when(pl.program_id(1) == 0)
    def _():
        s_ref[...] = jnp.zeros_like(s_ref)

    groups = _head_groups(GLA_HEADS, GLA_STACK)

    row, col = _chunk_masks()
    _, _, causal, _, _ = _stack_masks(GLA_STACK)

    def intra(ci):
        tok = _chunk_rows(ci)
        gate = _dot(la_ref[0, tok, :], wup_ref[...]) + bup_ref[...]
        bcum_all = _cumsum_rows(_log_sigmoid(gate) * (1.0 / GLA_TAU), row, col)
        out = []
        for heads in groups:
            q = _stack(lambda h: q_ref[0, tok, h * GLA_DK:(h + 1) * GLA_DK], heads) * GLA_DK ** -0.5
            k = _stack(lambda h: k_ref[0, tok, h * GLA_DK:(h + 1) * GLA_DK], heads)
            v = _stack(lambda h: v_ref[0, tok, h * GLA_DV:(h + 1) * GLA_DV], heads)
            bcum = _stack(lambda h: bcum_all[:, h * GLA_DK:(h + 1) * GLA_DK], heads)
            bcum_last = _stack_bcast(lambda h: bcum_all[CHUNK - 1:CHUNK, h * GLA_DK:(h + 1) * GLA_DK], heads)
            ends = [jnp.exp(bcum_all[:, h * GLA_DK:(h + 1) * GLA_DK].T[:, CHUNK - 1:CHUNK]) for h in heads]
            out.append((_gla_intra(q, k, v, bcum, bcum_last, GLA_STACK, causal), ends))
        return out

    for ci in range(tb // CHUNK):
        tok = _chunk_rows(ci)
        pre = intra(ci)
        for g, heads in enumerate(groups):
            parts, ends = pre[g]
            o, new_states = _gla_recur(*parts, ends, [s_ref[h] for h in heads])
            for j, h in enumerate(heads):
                s_ref[h] = new_states[j]
            o = o * lax.rsqrt(jnp.mean(o * o, axis=-1, keepdims=True) + EPS) * normw_ref[...]
            for j, h in enumerate(heads):
                rr = r_ref[0, tok, h * GLA_DV:(h + 1) * GLA_DV]
                o_ref[0, tok, h * GLA_DV:(h + 1) * GLA_DV] = (
                    o[_head_rows(j)] * (rr * _sigmoid(rr))).astype(o_ref.dtype)


def _gla_mixer(y, yla, w_up_pad, b_up, norm_w):
    b, t, _ = y.shape
    tb = min(MIX_BLOCK, t)
    qk_w = GLA_HEADS * GLA_DK
    base = 2 * RET_HEADS * RET_DK + 2 * GROUP_WIDTH
    vblk = (base + 2 * qk_w) // GROUP_WIDTH
    return _mixer_call(
        functools.partial(_gla_kernel, tb=tb),
        [_tok_spec(tb, qk_w, base // qk_w), _tok_spec(tb, qk_w, base // qk_w + 1),
         _tok_spec(tb, GROUP_WIDTH, vblk), _tok_spec(tb, GROUP_WIDTH, vblk + 1), _tok_spec(tb, LANES, 0)],
        [_full_spec((LANES, qk_w)), _full_spec((1, qk_w)), _full_spec((1, GLA_DV))],
        [pltpu.VMEM((GLA_HEADS, GLA_DK, GLA_DV), F32)],
        (y, y, y, y, yla, w_up_pad, b_up, norm_w), b, t, tb, "gla_mixer")


def _pad_lanes(v, offset=0):
    return jnp.zeros((1, LANES), F32).at[0, offset:offset + v.shape[0]].set(v.astype(F32))


def _even_mixer(xb, w_in_all, j, conv_w, a_log, dt_bias, gdn_norm_w, ml_gate_b, ml_norm_w, w_out_all):
    b, t, d = xb.shape
    x2 = xb.reshape(b * t, d)
    g_end = GDN_QKV + GROUP_WIDTH
    m_start = g_end + 2 * GDN_HEADS
    m_end = m_start + EVEN_MAIN - g_end
    w_in = w_in_all[j]
    w_main = jnp.concatenate([w_in[:, :g_end], w_in[:, m_start:m_end]], axis=1).astype(BF16)
    w_gate = jnp.concatenate([w_in[:, g_end:m_start], w_in[:, m_end:]], axis=1)
    w_gate = jnp.pad(w_gate, ((0, 0), (0, LANES - w_gate.shape[1]))).astype(BF16)
    w_out = _cast_bf16(w_out_all, j, "even_w_out_cast")
    y = _matmul(x2, w_main, out_dtype=F32, tm=1024, tn=1024, tk=d, name="even_in_proj").reshape(b, t, EVEN_MAIN)
    yg = _matmul(x2, w_gate, out_dtype=F32, tm=1024, tn=LANES, tk=d, name="even_gate_proj").reshape(b, t, LANES)
    o_a = _gdn_mixer(y, yg, conv_w.astype(F32), _pad_lanes(a_log, GA_OFF), _pad_lanes(dt_bias, GA_OFF),
                     gdn_norm_w.reshape(1, GDN_D).astype(F32))
    o_b = _mlstm_mixer(y, yg, _pad_lanes(ml_gate_b, MI_OFF), ml_norm_w.reshape(1, ML_DV).astype(F32))
    return _matmul_pair(o_a.reshape(b * t, GROUP_WIDTH), o_b.reshape(b * t, GROUP_WIDTH), w_out,
                        out_dtype=BRANCH_DTYPE, tm=1024, tn=1024, name="even_out_proj")


def _odd_mixer(xb, positions, w_in_all, j, gla_w_up, gla_b_up, gla_norm_w, w_out_all):
    b, t, d = xb.shape
    x2 = xb.reshape(b * t, d)
    w_in = w_in_all[j]
    w_main = w_in[:, :ODD_MAIN].astype(BF16)
    w_la = jnp.pad(w_in[:, ODD_MAIN:], ((0, 0), (0, LANES - GLA_RANK))).astype(BF16)
    w_out = _cast_bf16(w_out_all, j, "odd_w_out_cast")
    y = _matmul(x2, w_main, out_dtype=F32, tm=1024, tn=1024, tk=d, name="odd_in_proj").reshape(b, t, ODD_MAIN)
    yla = _matmul(x2, w_la, out_dtype=F32, tm=1024, tn=LANES, tk=d, name="odd_gate_proj").reshape(b, t, LANES)
    inv_freq = 1.0 / (ROPE_BASE ** jnp.linspace(0.0, 1.0, RET_DK // 2, dtype=F32))
    theta = positions.astype(F32)[:, :, None] * inv_freq
    cos, sin = jnp.cos(theta), jnp.sin(theta)
    cos2 = jnp.concatenate([cos, cos], axis=-1)
    sin2 = jnp.concatenate([-sin, sin], axis=-1)
    o_c = _ret_mixer(y, cos2, sin2)
    w_up_pad = jnp.pad(gla_w_up.astype(F32), ((0, LANES - GLA_RANK), (0, 0)))
    o_d = _gla_mixer(y, yla, w_up_pad, gla_b_up.reshape(1, -1).astype(F32),
                     gla_norm_w.reshape(1, GLA_DV).astype(F32))
    return _matmul_pair(o_c.reshape(b * t, GROUP_WIDTH), o_d.reshape(b * t, GROUP_WIDTH), w_out,
                        out_dtype=BRANCH_DTYPE, tm=1024, tn=1024, name="odd_out_proj")


def _mlp(xb2, w_up_all, w_down_all, layer):
    w_up = _cast_bf16(w_up_all, layer, f"mlp_w_up_cast_{layer}")
    w_down = _cast_bf16(w_down_all, layer, f"mlp_w_down_cast_{layer}")
    hdn = _matmul(xb2, w_up, out_dtype=BF16, tm=1024, tn=1024, tk=D_MODEL, relu2=True, name=f"mlp_up_{layer}")
    return _matmul(hdn, w_down, out_dtype=BRANCH_DTYPE, tm=1024, tn=1024, tk=4096, name=f"mlp_down_{layer}")


def kernel(x, positions, e_w_in, e_conv_w, e_a_log, e_dt_bias, e_gdn_norm_w, e_mlstm_gate_b, e_mlstm_norm_w, e_w_out, o_w_in, o_gla_w_up, o_gla_b_up, o_gla_norm_w, o_w_out, ln_mix_g, ln_mix_b, mlp_w_up, mlp_w_down, ln_mlp_g, ln_mlp_b):
    b, t, d = x.shape
    x2 = x.reshape(b * t, d)
    xb2 = x2.astype(BF16)
    for layer in range(DEPTH):
        j = layer // 2
        xb = xb2.reshape(b, t, d)
        if layer % 2 == 0:
            h = _even_mixer(xb, e_w_in, j, e_conv_w[j], e_a_log[j], e_dt_bias[j], e_gdn_norm_w[j],
                            e_mlstm_gate_b[j], e_mlstm_norm_w[j], e_w_out)
        else:
            h = _odd_mixer(xb, positions, o_w_in, j, o_gla_w_up[j], o_gla_b_up[j], o_gla_norm_w[j], o_w_out)
        x2, xb2 = _ln_residual(x2, h, ln_mix_g[layer], ln_mix_b[layer], name=f"ln_mix_{layer}")
        f = _mlp(xb2, mlp_w_up, mlp_w_down, layer)
        x2, xb2 = _ln_residual(x2, f, ln_mlp_g[layer], ln_mlp_b[layer], name=f"ln_mlp_{layer}")
    return x2.reshape(b, t, d)
```

```python
import functools
import math

import jax
import jax.numpy as jnp
from jax import lax
from jax.experimental import pallas as pl
from jax.experimental.pallas import tpu as pltpu

F32 = jnp.float32
BF16 = jnp.bfloat16
HIGHEST = lax.Precision.HIGHEST

D_MODEL = 4096
DEPTH = 2
CHUNK = 64
D_FF = 4 * D_MODEL
GROUP_WIDTH = D_MODEL // 2
ALPHA = (2.0 * DEPTH) ** 0.25
EPS = 1e-6
LN_EPS = 1e-5

GDN_HEADS = 16
GDN_D = GROUP_WIDTH // GDN_HEADS
CONV_K = 4
GDN_QKV = 3 * GROUP_WIDTH
ML_HEADS = 8
ML_DV = GROUP_WIDTH // ML_HEADS
ML_DK = ML_DV // 2
RET_HEADS = 8
RET_DV = GROUP_WIDTH // RET_HEADS
RET_DK = RET_DV // 2
ROPE_BASE = 10000.0
GLA_HEADS = 4
GLA_DV = GROUP_WIDTH // GLA_HEADS
GLA_DK = GLA_DV // 2
GLA_RANK = 16
GLA_TAU = 16.0

GDN_STACK, ML_STACK, RET_STACK, GLA_STACK = 2, 4, 2, 2
GDN_BLOCK = 4 * CHUNK
MIX_BLOCK = 8 * CHUNK
LANES = 128
EVEN_MAIN = GDN_QKV + GROUP_WIDTH + 2 * ML_HEADS * ML_DK + 2 * GROUP_WIDTH
ODD_MAIN = 2 * RET_HEADS * RET_DK + 2 * GROUP_WIDTH + 2 * GLA_HEADS * GLA_DK + 2 * GROUP_WIDTH
GA_OFF, GB_OFF, MI_OFF, MF_OFF = 0, GDN_HEADS, 2 * GDN_HEADS, 2 * GDN_HEADS + ML_HEADS

BRANCH_DTYPE = BF16
VMEM_LIMIT = 56 * 1024 * 1024
CAST_BLOCK_BYTES = 8 * 1024 * 1024


def _sigmoid(x):
    return 1.0 / (1.0 + jnp.exp(-x))


def _softplus(x):
    return jnp.maximum(x, 0.0) + jnp.log1p(jnp.exp(-jnp.abs(x)))


def _log_sigmoid(x):
    return -_softplus(-x)


def _mxu(x):
    return x.astype(BF16)


def _dot(a, b):
    return jnp.dot(_mxu(a), _mxu(b), preferred_element_type=F32)


def _dot_nt(a, b):
    return lax.dot_general(_mxu(a), _mxu(b), (((1,), (1,)), ((), ())), preferred_element_type=F32)


def _dot_tn(a, b):
    return _dot(a.T, b)


def _chunk_masks():
    row = lax.broadcasted_iota(jnp.int32, (CHUNK, CHUNK), 0)
    col = lax.broadcasted_iota(jnp.int32, (CHUNK, CHUNK), 1)
    return row, col


def _cumsum_rows(x, row, col):
    tril = jnp.where(row >= col, 1.0, 0.0).astype(F32)
    return jnp.dot(tril, x, precision=HIGHEST, preferred_element_type=F32)


def _unit_lower_inverses(mats, row, col):
    eye = jnp.where(row == col, 1.0, 0.0).astype(F32)
    same = (row >> 1) == (col >> 1)
    invs = [eye - jnp.where(same, a, 0.0) for a in mats]
    shift = 2
    while (1 << (shift - 1)) < CHUNK:
        same2 = (row >> shift) == (col >> shift)
        off_mask = jnp.logical_and(same2, jnp.logical_not(same))
        inv_b = [_mxu(inv) for inv in invs]
        tmp = [_dot(jnp.where(off_mask, a, 0.0), ib) for a, ib in zip(mats, inv_b)]
        invs = [inv - _dot(ib, t) for inv, ib, t in zip(invs, inv_b, tmp)]
        same = same2
        shift += 1
    return invs


def _mm_kernel_single(a_ref, b_ref, o_ref, *, relu2):
    r = _dot(a_ref[...], b_ref[...])
    if relu2:
        r = jnp.square(jnp.maximum(r, 0.0))
    o_ref[...] = r.astype(o_ref.dtype)


def _mm_kernel_acc(a_ref, b_ref, o_ref, acc_ref, *, nk, relu2):
    k = pl.program_id(2)

    @pl.when(k == 0)
    def _():
        acc_ref[...] = jnp.zeros_like(acc_ref)

    acc_ref[...] += _dot(a_ref[...], b_ref[...])

    @pl.when(k == nk - 1)
    def _():
        r = acc_ref[...]
        if relu2:
            r = jnp.square(jnp.maximum(r, 0.0))
        o_ref[...] = r.astype(o_ref.dtype)


def _matmul(a, b, *, out_dtype, tm, tn, tk, relu2=False, name):
    m, k = a.shape
    _, n = b.shape
    tm, tn, tk = min(tm, m), min(tn, n), min(tk, k)
    assert m % tm == 0 and n % tn == 0 and k % tk == 0
    nk = k // tk
    if nk == 1:
        return pl.pallas_call(
            functools.partial(_mm_kernel_single, relu2=relu2),
            grid=(m // tm, n // tn),
            in_specs=[pl.BlockSpec((tm, k), lambda i, j: (i, 0)),
                      pl.BlockSpec((k, tn), lambda i, j: (0, j))],
            out_specs=pl.BlockSpec((tm, tn), lambda i, j: (i, j)),
            out_shape=jax.ShapeDtypeStruct((m, n), out_dtype),
            compiler_params=pltpu.CompilerParams(
                dimension_semantics=("parallel", "parallel"), vmem_limit_bytes=VMEM_LIMIT),
            name=name,
        )(a, b)
    return pl.pallas_call(
        functools.partial(_mm_kernel_acc, nk=nk, relu2=relu2),
        grid=(m // tm, n // tn, nk),
        in_specs=[pl.BlockSpec((tm, tk), lambda i, j, kk: (i, kk)),
                  pl.BlockSpec((tk, tn), lambda i, j, kk: (kk, j))],
        out_specs=pl.BlockSpec((tm, tn), lambda i, j, kk: (i, j)),
        out_shape=jax.ShapeDtypeStruct((m, n), out_dtype),
        scratch_shapes=[pltpu.VMEM((tm, tn), F32)],
        compiler_params=pltpu.CompilerParams(
            dimension_semantics=("parallel", "parallel", "arbitrary"), vmem_limit_bytes=VMEM_LIMIT),
        name=name,
    )(a, b)


def _mm_pair_kernel(a1_ref, a2_ref, b_ref, o_ref):
    k1 = a1_ref.shape[1]
    r = _dot(a1_ref[...], b_ref[0:k1, :]) + _dot(a2_ref[...], b_ref[k1:, :])
    o_ref[...] = r.astype(o_ref.dtype)


def _matmul_pair(a1, a2, b, *, out_dtype, tm, tn, name):
    m, k1 = a1.shape
    k2 = a2.shape[1]
    n = b.shape[1]
    tm, tn = min(tm, m), min(tn, n)
    assert m % tm == 0 and n % tn == 0 and b.shape[0] == k1 + k2
    return pl.pallas_call(
        _mm_pair_kernel,
        grid=(m // tm, n // tn),
        in_specs=[pl.BlockSpec((tm, k1), lambda i, j: (i, 0)),
                  pl.BlockSpec((tm, k2), lambda i, j: (i, 0)),
                  pl.BlockSpec((k1 + k2, tn), lambda i, j: (0, j))],
        out_specs=pl.BlockSpec((tm, tn), lambda i, j: (i, j)),
        out_shape=jax.ShapeDtypeStruct((m, n), out_dtype),
        compiler_params=pltpu.CompilerParams(
            dimension_semantics=("parallel", "parallel"), vmem_limit_bytes=VMEM_LIMIT),
        name=name,
    )(a1, a2, b)


def _cast_rows(rows, width):
    br = min(rows, max(8, (CAST_BLOCK_BYTES // (4 * width)) // 8 * 8))
    while rows % br:
        br -= 8
    return br


def _cast_kernel(w_ref, o_ref):
    o_ref[...] = w_ref[...].astype(o_ref.dtype)


def _cast_bf16(w, layer, name):
    _, r, c = w.shape
    br = _cast_rows(r, c)
    return pl.pallas_call(
        _cast_kernel,
        grid=(r // br,),
        in_specs=[pl.BlockSpec((None, br, c), lambda i: (layer, i, 0))],
        out_specs=pl.BlockSpec((br, c), lambda i: (i, 0)),
        out_shape=jax.ShapeDtypeStruct((r, c), BF16),
        compiler_params=pltpu.CompilerParams(dimension_semantics=("parallel",), vmem_limit_bytes=VMEM_LIMIT),
        name=name,
    )(w)


def _ln_kernel(x_ref, h_ref, g_ref, b_ref, o_ref, *maybe_ob_ref):
    t = ALPHA * x_ref[...] + h_ref[...].astype(F32)
    mu = jnp.mean(t, axis=-1, keepdims=True)
    d = t - mu
    var = jnp.mean(d * d, axis=-1, keepdims=True)
    r = d * lax.rsqrt(var + LN_EPS) * g_ref[...] + b_ref[...]
    o_ref[...] = r
    for ob_ref in maybe_ob_ref:
        ob_ref[...] = r.astype(BF16)


def _ln_residual(x, h, g, b, *, tm=256, with_bf16=True, name):
    m, d = x.shape
    tm = min(tm, m)
    row = pl.BlockSpec((tm, d), lambda i: (i, 0))
    vec = pl.BlockSpec((1, d), lambda i: (0, 0))
    out = pl.pallas_call(
        _ln_kernel,
        grid=(m // tm,),
        in_specs=[row, row, vec, vec],
        out_specs=[row, row] if with_bf16 else [row],
        out_shape=([jax.ShapeDtypeStruct((m, d), F32), jax.ShapeDtypeStruct((m, d), BF16)] if with_bf16
                   else [jax.ShapeDtypeStruct((m, d), F32)]),
        compiler_params=pltpu.CompilerParams(
            dimension_semantics=("parallel",), vmem_limit_bytes=VMEM_LIMIT),
        name=name,
    )(x, h, g.reshape(1, d), b.reshape(1, d))
    return (out[0], out[1]) if with_bf16 else (out[0], None)


def _stack_masks(stack):
    rows = stack * CHUNK
    srow = lax.broadcasted_iota(jnp.int32, (rows, rows), 0)
    scol = lax.broadcasted_iota(jnp.int32, (rows, rows), 1)
    same_head = (srow // CHUNK) == (scol // CHUNK)
    causal = jnp.logical_and(same_head, srow >= scol)
    strict = jnp.logical_and(same_head, srow > scol)
    pick = (lax.broadcasted_iota(jnp.int32, (stack, rows), 1) // CHUNK
            == lax.broadcasted_iota(jnp.int32, (stack, rows), 0))
    return srow, scol, causal, strict, pick


def _head_groups(n_heads, stack):
    return [range(g * stack, (g + 1) * stack) for g in range(n_heads // stack)]


def _stack(fn, heads):
    return jnp.concatenate([fn(h) for h in heads], axis=0)


def _stack_bcast(fn, heads):
    return jnp.concatenate([jnp.broadcast_to(fn(h), (CHUNK, fn(h).shape[1])) for h in heads], axis=0)


def _row_form(t_rows, pick):
    return jnp.sum(jnp.where(pick, jnp.concatenate([t_rows] * t_rows.shape[0], axis=1), 0.0),
                   axis=0, keepdims=True)


def _head_rows(j):
    return slice(j * CHUNK, (j + 1) * CHUNK)


def _chunk_rows(ci):
    return slice(ci * CHUNK, (ci + 1) * CHUNK)


def _mixer_call(kernel, y_specs, extra_specs, scratch, args, b, t, tb, name):
    return pl.pallas_call(
        kernel,
        grid=(b, t // tb),
        in_specs=y_specs + extra_specs,
        out_specs=pl.BlockSpec((1, tb, GROUP_WIDTH), lambda i, c: (i, c, 0)),
        out_shape=jax.ShapeDtypeStruct((b, t, GROUP_WIDTH), BF16),
        scratch_shapes=scratch,
        compiler_params=pltpu.CompilerParams(
            dimension_semantics=("arbitrary", "arbitrary"), vmem_limit_bytes=VMEM_LIMIT),
        name=name,
    )(*args)


def _tok_spec(tb, width, blk):
    return pl.BlockSpec((1, tb, width), lambda i, c: (i, c, blk))


def _full_spec(shape):
    return pl.BlockSpec(shape, lambda i, c: (0,) * len(shape))


def _gdn_parts(qkv_ref, z_ref, gate_ref, convw_ref, alog_ref, dtb_ref, normw_ref, o_ref,
               cbuf, cs_ref, s_ref, *, tb):
    hist = 8

    @pl.when(pl.program_id(1) == 0)
    def _():
        cbuf[0:hist, :] = jnp.zeros((hist, GDN_QKV), F32)
        s_ref[...] = jnp.zeros_like(s_ref)

    cbuf[hist:hist + tb, :] = qkv_ref[0]
    for off in range(0, GDN_QKV, LANES):
        acc = cbuf[hist:hist + tb, off:off + LANES] * convw_ref[CONV_K - 1:CONV_K, off:off + LANES]
        for s in range(1, CONV_K):
            acc = acc + (cbuf[hist - s:hist - s + tb, off:off + LANES]
                         * convw_ref[CONV_K - 1 - s:CONV_K - s, off:off + LANES])
        cs_ref[:, off:off + LANES] = acc * _sigmoid(acc)
    cbuf[0:hist, :] = cbuf[tb:tb + hist, :]

    groups = _head_groups(GDN_HEADS, GDN_STACK)

    row, col = _chunk_masks()
    srow, scol, causal, strict, pick = _stack_masks(GDN_STACK)

    def intra(ci):
        tok = _chunk_rows(ci)
        gates = gate_ref[0, tok, :]
        g_all = -jnp.exp(alog_ref[...]) * _softplus(gates + dtb_ref[...])
        beta_all = _sigmoid(gates)
        gam_all = _cumsum_rows(g_all, row, col)
        gam_t = gam_all.T

        qs, ks, vs, gam_cs, beta_cs, decays, a_mats = [], [], [], [], [], [], []
        for heads in groups:
            q = _stack(lambda h: cs_ref[tok, h * GDN_D:(h + 1) * GDN_D], heads)
            k = _stack(lambda h: cs_ref[tok, GROUP_WIDTH + h * GDN_D:GROUP_WIDTH + (h + 1) * GDN_D], heads)
            v = _stack(lambda h: cs_ref[tok, 2 * GROUP_WIDTH + h * GDN_D:2 * GROUP_WIDTH + (h + 1) * GDN_D], heads)
            q = q * (lax.rsqrt(jnp.sum(q * q, axis=-1, keepdims=True) + EPS) * GDN_D ** -0.5)
            k = k * lax.rsqrt(jnp.sum(k * k, axis=-1, keepdims=True) + EPS)
            gam_c = _stack(lambda h: gam_all[:, GA_OFF + h:GA_OFF + h + 1], heads)
            beta_c = _stack(lambda h: beta_all[:, GB_OFF + h:GB_OFF + h + 1], heads)
            gam_r = _row_form(gam_t[GA_OFF + heads[0]:GA_OFF + heads[0] + GDN_STACK, :], pick)
            decay = jnp.where(causal, jnp.exp(jnp.where(causal, gam_c - gam_r, 0.0)), 0.0)
            a_mats.append(jnp.where(strict, _dot_nt(k, k) * decay * beta_c, 0.0))
            qs.append(q); ks.append(k); vs.append(v)
            gam_cs.append(gam_c); beta_cs.append(beta_c); decays.append(decay)

        invs = _unit_lower_inverses(a_mats, srow, scol)
        out = []
        for g, heads in enumerate(groups):
            q, k, v, gam_c, beta_c = qs[g], ks[g], vs[g], gam_cs[g], beta_cs[g]
            egam = jnp.exp(gam_c)
            rhs = jnp.concatenate([v * beta_c, k * (beta_c * egam)], axis=1)
            sol = _dot(invs[g], rhs)
            qk = _dot_nt(q, k) * decays[g]
            gam_last = _stack_bcast(lambda h: gam_all[CHUNK - 1:CHUNK, GA_OFF + h:GA_OFF + h + 1], heads)
            k_end = k * jnp.exp(gam_last - gam_c)
            c_decs = [jnp.exp(gam_all[CHUNK - 1:CHUNK, GA_OFF + h:GA_OFF + h + 1]) for h in heads]
            out.append((sol, qk, q * egam, k_end, c_decs))
        return out

    def state_step(ci, pre):
        tok = _chunk_rows(ci)
        for g, heads in enumerate(groups):
            sol, qk, q_dec, k_end, c_decs = pre[g]
            v_new, q_state, states = [], [], []
            for j, h in enumerate(heads):
                sl = _head_rows(j)
                state = s_ref[h]
                both = _dot(jnp.concatenate([sol[sl, GDN_D:], q_dec[sl]], axis=0), state)
                v_new.append(sol[sl, :GDN_D] - both[:CHUNK])
                q_state.append(both[CHUNK:])
                states.append(state)
            o = jnp.concatenate(q_state, axis=0) + _dot(qk, jnp.concatenate(v_new, axis=0))
            for j, h in enumerate(heads):
                s_ref[h] = states[j] * c_decs[j] + _dot_tn(k_end[_head_rows(j)], v_new[j])
            o = o * lax.rsqrt(jnp.mean(o * o, axis=-1, keepdims=True) + EPS) * normw_ref[...]
            for j, h in enumerate(heads):
                zz = z_ref[0, tok, h * GDN_D:(h + 1) * GDN_D]
                o_ref[0, tok, h * GDN_D:(h + 1) * GDN_D] = (
                    o[_head_rows(j)] * (zz * _sigmoid(zz))).astype(o_ref.dtype)

    return intra, state_step


def _mlstm_parts(q_ref, k_ref, v_ref, og_ref, gate_ref, gb_ref, normw_ref, o_ref, c_ref, n_ref, m_ref):
    @pl.when(pl.program_id(1) == 0)
    def _():
        c_ref[...] = jnp.zeros_like(c_ref)
        n_ref[...] = jnp.zeros_like(n_ref)
        m_ref[...] = jnp.zeros_like(m_ref)

    groups = _head_groups(ML_HEADS, ML_STACK)
    row, col = _chunk_masks()
    _, _, causal, _, pick = _stack_masks(ML_STACK)

    m_sts = [_stack_bcast(lambda h: m_ref[h, 0:1, 0:1], heads) for heads in groups]

    def weights(ci):
        tok = _chunk_rows(ci)
        pre = gate_ref[0, tok, :] + gb_ref[...]
        fc_all = _cumsum_rows(_log_sigmoid(pre), row, col)
        fc_t = fc_all.T
        pre_t = pre.T
        per_group = []
        for g, heads in enumerate(groups):
            h0 = heads[0]
            q = _stack(lambda h: q_ref[0, tok, h * ML_DK:(h + 1) * ML_DK], heads)
            k = _stack(lambda h: k_ref[0, tok, h * ML_DK:(h + 1) * ML_DK], heads) * ML_DK ** -0.5
            v = _stack(lambda h: v_ref[0, tok, h * ML_DV:(h + 1) * ML_DV], heads)
            fc_c = _stack(lambda h: fc_all[:, MF_OFF + h:MF_OFF + h + 1], heads)
            ic_c = _stack(lambda h: pre[:, MI_OFF + h:MI_OFF + h + 1], heads)
            fc_last = _stack_bcast(lambda h: fc_all[CHUNK - 1:CHUNK, MF_OFF + h:MF_OFF + h + 1], heads)
            fc_r = _row_form(fc_t[MF_OFF + h0:MF_OFF + h0 + ML_STACK, :], pick)
            ic_r = _row_form(pre_t[MI_OFF + h0:MI_OFF + h0 + ML_STACK, :], pick)
            d_log = jnp.where(causal, fc_c - fc_r + ic_r, -jnp.inf)
            m_intra = jnp.max(d_log, axis=-1, keepdims=True)
            qk = _dot_nt(q, k)
            src_end = fc_last - fc_c + ic_c
            m_src = jnp.concatenate(
                [jnp.broadcast_to(jnp.max(src_end[_head_rows(j)], axis=0, keepdims=True), (CHUNK, 1))
                 for j in range(ML_STACK)], axis=0)
            m_st = m_sts[g]
            m_inter = fc_c + m_st
            m_t = jnp.maximum(m_inter, m_intra)
            w_inter = jnp.exp(m_inter - m_t)
            w_intra = jnp.exp(d_log - m_t) * qk
            m_new = jnp.maximum(fc_last + m_st, m_src)
            w_state = jnp.exp(fc_last + m_st - m_new)
            kw = k * jnp.exp(src_end - m_new)
            m_sts[g] = m_new
            per_group.append((q, w_inter, _dot(w_intra, v), jnp.sum(w_intra, axis=-1, keepdims=True),
                              jnp.exp(-m_t), w_state,
                              [_dot_tn(kw[_head_rows(j)], v[_head_rows(j)]) for j in range(ML_STACK)],
                              [jnp.sum(kw[_head_rows(j)], axis=0, keepdims=True) for j in range(ML_STACK)]))
        return per_group

    c_sts = [[c_ref[h] for h in heads] for heads in groups]
    n_sts = [[n_ref[h, 0:1, :] for h in heads] for heads in groups]

    def recur(ci, cur):
        tok = _chunk_rows(ci)
        for g, heads in enumerate(groups):
            q, w_inter, intra_num, intra_den, floor, w_state, kv, ksum = cur[g]
            q_c = jnp.concatenate([_dot(q[_head_rows(j)], c_sts[g][j]) for j in range(ML_STACK)], axis=0)
            n_rows = jnp.concatenate(
                [jnp.broadcast_to(n_sts[g][j], (CHUNK, ML_DK)) for j in range(ML_STACK)], axis=0)
            num = w_inter * q_c + intra_num
            den = w_inter * jnp.sum(q * n_rows, axis=-1, keepdims=True) + intra_den
            hh = num / jnp.maximum(jnp.abs(den), floor)
            hh = hh * lax.rsqrt(jnp.mean(hh * hh, axis=-1, keepdims=True) + EPS) * normw_ref[...]
            for j, h in enumerate(heads):
                gg = og_ref[0, tok, h * ML_DV:(h + 1) * ML_DV]
                o_ref[0, tok, h * ML_DV:(h + 1) * ML_DV] = (hh[_head_rows(j)] * _sigmoid(gg)).astype(o_ref.dtype)
                ws = w_state[j * CHUNK:j * CHUNK + 1, :]
                c_sts[g][j] = ws * c_sts[g][j] + kv[j]
                n_sts[g][j] = ws * n_sts[g][j] + ksum[j]

    def finish():
        for g, heads in enumerate(groups):
            for j, h in enumerate(heads):
                c_ref[h] = c_sts[g][j]
                n_ref[h] = jnp.broadcast_to(n_sts[g][j], (8, ML_DK))
                m_ref[h] = jnp.broadcast_to(m_sts[g][j * CHUNK:j * CHUNK + 1, :], (8, LANES))

    return weights, recur, finish


def _pipelined_chunks(n_chunks, prepare, consume):
    cur = prepare(0)
    for ci in range(n_chunks):
        nxt = prepare(ci + 1) if ci + 1 < n_chunks else None
        consume(ci, cur)
        cur = nxt


def _gdn_kernel(*refs, tb):
    intra, state_step = _gdn_parts(*refs, tb=tb)
    _pipelined_chunks(tb // CHUNK, intra, state_step)


def _mlstm_kernel(*refs, tb):
    weights, recur, finish = _mlstm_parts(*refs)
    _pipelined_chunks(tb // CHUNK, weights, recur)
    finish()


def _gdn_mixer(y, yg, conv_w, alog_row, dtb_row, norm_w):
    b, t, _ = y.shape
    tb = min(GDN_BLOCK, t)
    return _mixer_call(
        functools.partial(_gdn_kernel, tb=tb),
        [_tok_spec(tb, GDN_QKV, 0), _tok_spec(tb, GROUP_WIDTH, GDN_QKV // GROUP_WIDTH), _tok_spec(tb, LANES, 0)],
        [_full_spec((CONV_K, GDN_QKV)), _full_spec((1, LANES)), _full_spec((1, LANES)), _full_spec((1, GDN_D))],
        [pltpu.VMEM((tb + 8, GDN_QKV), F32), pltpu.VMEM((tb, GDN_QKV), F32),
         pltpu.VMEM((GDN_HEADS, GDN_D, GDN_D), F32)],
        (y, y, yg, conv_w, alog_row, dtb_row, norm_w), b, t, tb, "gdn_mixer")


def _mlstm_mixer(y, yg, gb_row, norm_w):
    b, t, _ = y.shape
    tb = min(MIX_BLOCK, t)
    qk_w = ML_HEADS * ML_DK
    base = GDN_QKV + GROUP_WIDTH
    vblk = (base + 2 * qk_w) // GROUP_WIDTH
    return _mixer_call(
        functools.partial(_mlstm_kernel, tb=tb),
        [_tok_spec(tb, qk_w, base // qk_w), _tok_spec(tb, qk_w, base // qk_w + 1),
         _tok_spec(tb, GROUP_WIDTH, vblk), _tok_spec(tb, GROUP_WIDTH, vblk + 1), _tok_spec(tb, LANES, 0)],
        [_full_spec((1, LANES)), _full_spec((1, ML_DV))],
        [pltpu.VMEM((ML_HEADS, ML_DK, ML_DV), F32), pltpu.VMEM((ML_HEADS, 8, ML_DK), F32),
         pltpu.VMEM((ML_HEADS, 8, LANES), F32)],
        (y, y, y, y, yg, gb_row, norm_w), b, t, tb, "mlstm_mixer")


def _gla_intra(q, k, v, bcum, bcum_last, stack, causal):
    q_dec = q * jnp.exp(bcum)
    k_inv = k * jnp.exp(-bcum)
    attn = jnp.where(causal, _dot_nt(q_dec, k_inv), 0.0)
    o_intra = _dot(attn, v)
    k_end = k * jnp.exp(bcum_last - bcum)
    kv = [_dot_tn(k_end[_head_rows(j)], v[_head_rows(j)]) for j in range(stack)]
    return q_dec, o_intra, kv


def _gla_recur(q_dec, o_intra, kv, end_decay_cols, states):
    o = o_intra + jnp.concatenate(
        [_dot(q_dec[_head_rows(j)], states[j]) for j in range(len(states))], axis=0)
    return o, [states[j] * end_decay_cols[j] + kv[j] for j in range(len(states))]


def _ret_kernel(q_ref, k_ref, v_ref, g_ref, cos_ref, sin_ref, o_ref, s_ref, *, tb):
    @pl.when(pl.program_id(1) == 0)
    def _():
        s_ref[...] = jnp.zeros_like(s_ref)

    groups = _head_groups(RET_HEADS, RET_STACK)

    _, _, causal, _, _ = _stack_masks(RET_STACK)
    steps = (lax.broadcasted_iota(jnp.int32, (CHUNK, 1), 0) + 1).astype(F32)

    def intra(ci, heads):
        tok = _chunk_rows(ci)
        cos2 = jnp.concatenate([cos_ref[0, tok, :]] * RET_STACK, axis=0)
        sin2 = jnp.concatenate([sin_ref[0, tok, :]] * RET_STACK, axis=0)
        log_gamma = [math.log(1.0 - 2.0 ** (-5.0 - h)) for h in heads]
        q = _stack(lambda h: q_ref[0, tok, h * RET_DK:(h + 1) * RET_DK], heads)
        k = _stack(lambda h: k_ref[0, tok, h * RET_DK:(h + 1) * RET_DK], heads)
        v = _stack(lambda h: v_ref[0, tok, h * RET_DV:(h + 1) * RET_DV], heads)
        q = q * cos2 + pltpu.roll(q, RET_DK // 2, 1) * sin2
        k = (k * cos2 + pltpu.roll(k, RET_DK // 2, 1) * sin2) * RET_DK ** -0.5
        bcum = jnp.concatenate([steps * lg for lg in log_gamma], axis=0)
        bcum_last = jnp.concatenate([jnp.full((CHUNK, 1), CHUNK * lg, F32) for lg in log_gamma], axis=0)
        return _gla_intra(q, k, v, bcum, bcum_last, RET_STACK, causal)

    n_chunks = tb // CHUNK
    pre = [[intra(ci, heads) for heads in groups] for ci in range(n_chunks)]
    for g, heads in enumerate(groups):
        ends = [jnp.exp(jnp.full((1, 1), CHUNK * math.log(1.0 - 2.0 ** (-5.0 - h)), F32)) for h in heads]
        states = [s_ref[h] for h in heads]
        for ci in range(n_chunks):
            tok = _chunk_rows(ci)
            o, states = _gla_recur(*pre[ci][g], ends, states)
            o = o * lax.rsqrt(jnp.mean(o * o, axis=-1, keepdims=True) + EPS)
            for j, h in enumerate(heads):
                gg = g_ref[0, tok, h * RET_DV:(h + 1) * RET_DV]
                o_ref[0, tok, h * RET_DV:(h + 1) * RET_DV] = (
                    o[_head_rows(j)] * (gg * _sigmoid(gg))).astype(o_ref.dtype)
        for j, h in enumerate(heads):
            s_ref[h] = states[j]


def _ret_mixer(y, cos2, sin2):
    b, t, _ = y.shape
    tb = min(MIX_BLOCK, t)
    qk_w = RET_HEADS * RET_DK
    return _mixer_call(
        functools.partial(_ret_kernel, tb=tb),
        [_tok_spec(tb, qk_w, 0), _tok_spec(tb, qk_w, 1), _tok_spec(tb, GROUP_WIDTH, 1),
         _tok_spec(tb, GROUP_WIDTH, 2), _tok_spec(tb, RET_DK, 0), _tok_spec(tb, RET_DK, 0)],
        [],
        [pltpu.VMEM((RET_HEADS, RET_DK, RET_DV), F32)],
        (y, y, y, y, cos2, sin2), b, t, tb, "ret_mixer")


def _gla_kernel(q_ref, k_ref, v_ref, r_ref, la_ref, wup_ref, bup_ref, normw_ref, o_ref, s_ref, *, tb):
    @pl.when(pl.program_id(1) == 0)
    def _():
        s_ref[...] = jnp.zeros_like(s_ref)

    groups = _head_groups(GLA_HEADS, GLA_STACK)

    row, col = _chunk_masks()
    _, _, causal, _, _ = _stack_masks(GLA_STACK)

    def intra(ci):
        tok = _chunk_rows(ci)
        gate = _dot(la_ref[0, tok, :], wup_ref[...]) + bup_ref[...]
        bcum_all = _cumsum_rows(_log_sigmoid(gate) * (1.0 / GLA_TAU), row, col)
        out = []
        for heads in groups:
            q = _stack(lambda h: q_ref[0, tok, h * GLA_DK:(h + 1) * GLA_DK], heads) * GLA_DK ** -0.5
            k = _stack(lambda h: k_ref[0, tok, h * GLA_DK:(h + 1) * GLA_DK], heads)
            v = _stack(lambda h: v_ref[0, tok, h * GLA_DV:(h + 1) * GLA_DV], heads)
            bcum = _stack(lambda h: bcum_all[:, h * GLA_DK:(h + 1) * GLA_DK], heads)
            bcum_last = _stack_bcast(lambda h: bcum_all[CHUNK - 1:CHUNK, h * GLA_DK:(h + 1) * GLA_DK], heads)
            ends = [jnp.exp(bcum_all[:, h * GLA_DK:(h + 1) * GLA_DK].T[:, CHUNK - 1:CHUNK]) for h in heads]
            out.append((_gla_intra(q, k, v, bcum, bcum_last, GLA_STACK, causal), ends))
        return out

    for ci in range(tb // CHUNK):
        tok = _chunk_rows(ci)
        pre = intra(ci)
        for g, heads in enumerate(groups):
            parts, ends = pre[g]
            o, new_states = _gla_recur(*parts, ends, [s_ref[h] for h in heads])
            for j, h in enumerate(heads):
                s_ref[h] = new_states[j]
            o = o * lax.rsqrt(jnp.mean(o * o, axis=-1, keepdims=True) + EPS) * normw_ref[...]
            for j, h in enumerate(heads):
                rr = r_ref[0, tok, h * GLA_DV:(h + 1) * GLA_DV]
                o_ref[0, tok, h * GLA_DV:(h + 1) * GLA_DV] = (
                    o[_head_rows(j)] * (rr * _sigmoid(rr))).astype(o_ref.dtype)


def _gla_mixer(y, yla, w_up_pad, b_up, norm_w):
    b, t, _ = y.shape
    tb = min(MIX_BLOCK, t)
    qk_w = GLA_HEADS * GLA_DK
    base = 2 * RET_HEADS * RET_DK + 2 * GROUP_WIDTH
    vblk = (base + 2 * qk_w) // GROUP_WIDTH
    return _mixer_call(
        functools.partial(_gla_kernel, tb=tb),
        [_tok_spec(tb, qk_w, base // qk_w), _tok_spec(tb, qk_w, base // qk_w + 1),
         _tok_spec(tb, GROUP_WIDTH, vblk), _tok_spec(tb, GROUP_WIDTH, vblk + 1), _tok_spec(tb, LANES, 0)],
        [_full_spec((LANES, qk_w)), _full_spec((1, qk_w)), _full_spec((1, GLA_DV))],
        [pltpu.VMEM((GLA_HEADS, GLA_DK, GLA_DV), F32)],
        (y, y, y, y, yla, w_up_pad, b_up, norm_w), b, t, tb, "gla_mixer")


def _pad_lanes(v, offset=0):
    return jnp.zeros((1, LANES), F32).at[0, offset:offset + v.shape[0]].set(v.astype(F32))


def _even_mixer(xb, w_in_all, j, conv_w, a_log, dt_bias, gdn_norm_w, ml_gate_b, ml_norm_w, w_out_all):
    b, t, d = xb.shape
    x2 = xb.reshape(b * t, d)
    g_end = GDN_QKV + GROUP_WIDTH
    m_start = g_end + 2 * GDN_HEADS
    m_end = m_start + EVEN_MAIN - g_end
    w_in = w_in_all[j]
    w_main = jnp.concatenate([w_in[:, :g_end], w_in[:, m_start:m_end]], axis=1).astype(BF16)
    w_gate = jnp.concatenate([w_in[:, g_end:m_start], w_in[:, m_end:]], axis=1)
    w_gate = jnp.pad(w_gate, ((0, 0), (0, LANES - w_gate.shape[1]))).astype(BF16)
    w_out = _cast_bf16(w_out_all, j, "even_w_out_cast")
    y = _matmul(x2, w_main, out_dtype=F32, tm=1024, tn=1024, tk=d, name="even_in_proj").reshape(b, t, EVEN_MAIN)
    yg = _matmul(x2, w_gate, out_dtype=F32, tm=1024, tn=LANES, tk=d, name="even_gate_proj").reshape(b, t, LANES)
    o_a = _gdn_mixer(y, yg, conv_w.astype(F32), _pad_lanes(a_log, GA_OFF), _pad_lanes(dt_bias, GA_OFF),
                     gdn_norm_w.reshape(1, GDN_D).astype(F32))
    o_b = _mlstm_mixer(y, yg, _pad_lanes(ml_gate_b, MI_OFF), ml_norm_w.reshape(1, ML_DV).astype(F32))
    return _matmul_pair(o_a.reshape(b * t, GROUP_WIDTH), o_b.reshape(b * t, GROUP_WIDTH), w_out,
                        out_dtype=BRANCH_DTYPE, tm=1024, tn=1024, name="even_out_proj")


def _odd_mixer(xb, positions, w_in_all, j, gla_w_up, gla_b_up, gla_norm_w, w_out_all):
    b, t, d = xb.shape
    x2 = xb.reshape(b * t, d)
    w_in = w_in_all[j]
    w_main = w_in[:, :ODD_MAIN].astype(BF16)
    w_la = jnp.pad(w_in[:, ODD_MAIN:], ((0, 0), (0, LANES - GLA_RANK))).astype(BF16)
    w_out = _cast_bf16(w_out_all, j, "odd_w_out_cast")
    y = _matmul(x2, w_main, out_dtype=F32, tm=1024, tn=1024, tk=d, name="odd_in_proj").reshape(b, t, ODD_MAIN)
    yla = _matmul(x2, w_la, out_dtype=F32, tm=1024, tn=LANES, tk=d, name="odd_gate_proj").reshape(b, t, LANES)
    inv_freq = 1.0 / (ROPE_BASE ** jnp.linspace(0.0, 1.0, RET_DK // 2, dtype=F32))
    theta = positions.astype(F32)[:, :, None] * inv_freq
    cos, sin = jnp.cos(theta), jnp.sin(theta)
    cos2 = jnp.concatenate([cos, cos], axis=-1)
    sin2 = jnp.concatenate([-sin, sin], axis=-1)
    o_c = _ret_mixer(y, cos2, sin2)
    w_up_pad = jnp.pad(gla_w_up.astype(F32), ((0, LANES - GLA_RANK), (0, 0)))
    o_d = _gla_mixer(y, yla, w_up_pad, gla_b_up.reshape(1, -1).astype(F32),
                     gla_norm_w.reshape(1, GLA_DV).astype(F32))
    return _matmul_pair(o_c.reshape(b * t, GROUP_WIDTH), o_d.reshape(b * t, GROUP_WIDTH), w_out,
                        out_dtype=BRANCH_DTYPE, tm=1024, tn=1024, name="odd_out_proj")


def _mlp(xb2, w_up_all, w_down_all, layer):
    w_up = _cast_bf16(w_up_all, layer, f"mlp_w_up_cast_{layer}")
    w_down = _cast_bf16(w_down_all, layer, f"mlp_w_down_cast_{layer}")
    hdn = _matmul(xb2, w_up, out_dtype=BF16, tm=1024, tn=1024, tk=D_MODEL, relu2=True, name=f"mlp_up_{layer}")
    return _matmul(hdn, w_down, out_dtype=BRANCH_DTYPE, tm=1024, tn=1024, tk=4096, name=f"mlp_down_{layer}")


def kernel(x, positions, e_w_in, e_conv_w, e_a_log, e_dt_bias, e_gdn_norm_w, e_mlstm_gate_b, e_mlstm_norm_w, e_w_out, o_w_in, o_gla_w_up, o_gla_b_up, o_gla_norm_w, o_w_out, ln_mix_g, ln_mix_b, mlp_w_up, mlp_w_down, ln_mlp_g, ln_mlp_b):
    b, t, d = x.shape
    x2 = x.reshape(b * t, d)
    xb2 = x2.astype(BF16)
    for layer in range(DEPTH):
        j = layer // 2
        xb = xb2.reshape(b, t, d)
        if layer % 2 == 0:
            h = _even_mixer(xb, e_w_in, j, e_conv_w[j], e_a_log[j], e_dt_bias[j], e_gdn_norm_w[j],
                            e_mlstm_gate_b[j], e_mlstm_norm_w[j], e_w_out)
        else:
            h = _odd_mixer(xb, positions, o_w_in, j, o_gla_w_up[j], o_gla_b_up[j], o_gla_norm_w[j], o_w_out)
        x2, xb2 = _ln_residual(x2, h, ln_mix_g[layer], ln_mix_b[layer], name=f"ln_mix_{layer}")
        f = _mlp(xb2, mlp_w_up, mlp_w_down, layer)
        x2, xb2 = _ln_residual(x2, f, ln_mlp_g[layer], ln_mlp_b[layer], with_bf16=layer + 1 < DEPTH,
                               name=f"ln_mlp_{layer}")
    return x2.reshape(b, t, d)
```

```python
import functools
import math

import jax
import jax.numpy as jnp
from jax import lax
from jax.experimental import pallas as pl
from jax.experimental.pallas import tpu as pltpu

F32 = jnp.float32
BF16 = jnp.bfloat16
HIGHEST = lax.Precision.HIGHEST

D_MODEL = 4096
DEPTH = 2
CHUNK = 64
D_FF = 4 * D_MODEL
GROUP_WIDTH = D_MODEL // 2
ALPHA = (2.0 * DEPTH) ** 0.25
EPS = 1e-6
LN_EPS = 1e-5

GDN_HEADS = 16
GDN_D = GROUP_WIDTH // GDN_HEADS
CONV_K = 4
GDN_QKV = 3 * GROUP_WIDTH
ML_HEADS = 8
ML_DV = GROUP_WIDTH // ML_HEADS
ML_DK = ML_DV // 2
RET_HEADS = 8
RET_DV = GROUP_WIDTH // RET_HEADS
RET_DK = RET_DV // 2
ROPE_BASE = 10000.0
GLA_HEADS = 4
GLA_DV = GROUP_WIDTH // GLA_HEADS
GLA_DK = GLA_DV // 2
GLA_RANK = 16
GLA_TAU = 16.0

GDN_STACK, ML_STACK, RET_STACK, GLA_STACK = 2, 4, 2, 4
GDN_BLOCK = 4 * CHUNK
MIX_BLOCK = 8 * CHUNK
GLA_BLOCK = 4 * CHUNK
LANES = 128
EVEN_MAIN = GDN_QKV + GROUP_WIDTH + 2 * ML_HEADS * ML_DK + 2 * GROUP_WIDTH
ODD_MAIN = 2 * RET_HEADS * RET_DK + 2 * GROUP_WIDTH + 2 * GLA_HEADS * GLA_DK + 2 * GROUP_WIDTH
GA_OFF, GB_OFF, MI_OFF, MF_OFF = 0, GDN_HEADS, 2 * GDN_HEADS, 2 * GDN_HEADS + ML_HEADS

BRANCH_DTYPE = BF16
VMEM_LIMIT = 56 * 1024 * 1024
CAST_BLOCK_BYTES = 8 * 1024 * 1024


def _sigmoid(x):
    return 1.0 / (1.0 + jnp.exp(-x))


def _softplus(x):
    return jnp.maximum(x, 0.0) + jnp.log1p(jnp.exp(-jnp.abs(x)))


def _log_sigmoid(x):
    return -_softplus(-x)


def _mxu(x):
    return x.astype(BF16)


def _dot(a, b):
    return jnp.dot(_mxu(a), _mxu(b), preferred_element_type=F32)


def _dot_nt(a, b):
    return lax.dot_general(_mxu(a), _mxu(b), (((1,), (1,)), ((), ())), preferred_element_type=F32)


def _dot_tn(a, b):
    return _dot(a.T, b)


def _chunk_masks():
    row = lax.broadcasted_iota(jnp.int32, (CHUNK, CHUNK), 0)
    col = lax.broadcasted_iota(jnp.int32, (CHUNK, CHUNK), 1)
    return row, col


def _cumsum_rows(x, row, col):
    tril = jnp.where(row >= col, 1.0, 0.0).astype(F32)
    return jnp.dot(tril, x, precision=HIGHEST, preferred_element_type=F32)


def _unit_lower_inverses(mats, row, col):
    eye = jnp.where(row == col, 1.0, 0.0).astype(F32)
    same = (row >> 1) == (col >> 1)
    invs = [eye - jnp.where(same, a, 0.0) for a in mats]
    shift = 2
    while (1 << (shift - 1)) < CHUNK:
        same2 = (row >> shift) == (col >> shift)
        off_mask = jnp.logical_and(same2, jnp.logical_not(same))
        inv_b = [_mxu(inv) for inv in invs]
        tmp = [_dot(jnp.where(off_mask, a, 0.0), ib) for a, ib in zip(mats, inv_b)]
        invs = [inv - _dot(ib, t) for inv, ib, t in zip(invs, inv_b, tmp)]
        same = same2
        shift += 1
    return invs


def _mm_kernel_single(a_ref, b_ref, o_ref, *, relu2):
    r = _dot(a_ref[...], b_ref[...])
    if relu2:
        r = jnp.square(jnp.maximum(r, 0.0))
    o_ref[...] = r.astype(o_ref.dtype)


def _mm_kernel_acc(a_ref, b_ref, o_ref, acc_ref, *, nk, relu2):
    k = pl.program_id(2)

    @pl.when(k == 0)
    def _():
        acc_ref[...] = jnp.zeros_like(acc_ref)

    acc_ref[...] += _dot(a_ref[...], b_ref[...])

    @pl.when(k == nk - 1)
    def _():
        r = acc_ref[...]
        if relu2:
            r = jnp.square(jnp.maximum(r, 0.0))
        o_ref[...] = r.astype(o_ref.dtype)


def _matmul(a, b, *, out_dtype, tm, tn, tk, relu2=False, name):
    m, k = a.shape
    _, n = b.shape
    tm, tn, tk = min(tm, m), min(tn, n), min(tk, k)
    assert m % tm == 0 and n % tn == 0 and k % tk == 0
    nk = k // tk
    if nk == 1:
        return pl.pallas_call(
            functools.partial(_mm_kernel_single, relu2=relu2),
            grid=(m // tm, n // tn),
            in_specs=[pl.BlockSpec((tm, k), lambda i, j: (i, 0)),
                      pl.BlockSpec((k, tn), lambda i, j: (0, j))],
            out_specs=pl.BlockSpec((tm, tn), lambda i, j: (i, j)),
            out_shape=jax.ShapeDtypeStruct((m, n), out_dtype),
            compiler_params=pltpu.CompilerParams(
                dimension_semantics=("parallel", "parallel"), vmem_limit_bytes=VMEM_LIMIT),
            name=name,
        )(a, b)
    return pl.pallas_call(
        functools.partial(_mm_kernel_acc, nk=nk, relu2=relu2),
        grid=(m // tm, n // tn, nk),
        in_specs=[pl.BlockSpec((tm, tk), lambda i, j, kk: (i, kk)),
                  pl.BlockSpec((tk, tn), lambda i, j, kk: (kk, j))],
        out_specs=pl.BlockSpec((tm, tn), lambda i, j, kk: (i, j)),
        out_shape=jax.ShapeDtypeStruct((m, n), out_dtype),
        scratch_shapes=[pltpu.VMEM((tm, tn), F32)],
        compiler_params=pltpu.CompilerParams(
            dimension_semantics=("parallel", "parallel", "arbitrary"), vmem_limit_bytes=VMEM_LIMIT),
        name=name,
    )(a, b)


def _mm_pair_kernel(a1_ref, a2_ref, b_ref, o_ref):
    k1 = a1_ref.shape[1]
    r = _dot(a1_ref[...], b_ref[0:k1, :]) + _dot(a2_ref[...], b_ref[k1:, :])
    o_ref[...] = r.astype(o_ref.dtype)


def _matmul_pair(a1, a2, b, *, out_dtype, tm, tn, name):
    m, k1 = a1.shape
    k2 = a2.shape[1]
    n = b.shape[1]
    tm, tn = min(tm, m), min(tn, n)
    assert m % tm == 0 and n % tn == 0 and b.shape[0] == k1 + k2
    return pl.pallas_call(
        _mm_pair_kernel,
        grid=(m // tm, n // tn),
        in_specs=[pl.BlockSpec((tm, k1), lambda i, j: (i, 0)),
                  pl.BlockSpec((tm, k2), lambda i, j: (i, 0)),
                  pl.BlockSpec((k1 + k2, tn), lambda i, j: (0, j))],
        out_specs=pl.BlockSpec((tm, tn), lambda i, j: (i, j)),
        out_shape=jax.ShapeDtypeStruct((m, n), out_dtype),
        compiler_params=pltpu.CompilerParams(
            dimension_semantics=("parallel", "parallel"), vmem_limit_bytes=VMEM_LIMIT),
        name=name,
    )(a1, a2, b)


def _cast_rows(rows, width):
    br = min(rows, max(8, (CAST_BLOCK_BYTES // (4 * width)) // 8 * 8))
    while rows % br:
        br -= 8
    return br


def _cast_kernel(w_ref, o_ref):
    o_ref[...] = w_ref[...].astype(o_ref.dtype)


def _cast_bf16(w, layer, name):
    _, r, c = w.shape
    br = _cast_rows(r, c)
    return pl.pallas_call(
        _cast_kernel,
        grid=(r // br,),
        in_specs=[pl.BlockSpec((None, br, c), lambda i: (layer, i, 0))],
        out_specs=pl.BlockSpec((br, c), lambda i: (i, 0)),
        out_shape=jax.ShapeDtypeStruct((r, c), BF16),
        compiler_params=pltpu.CompilerParams(dimension_semantics=("parallel",), vmem_limit_bytes=VMEM_LIMIT),
        name=name,
    )(w)


def _ln_kernel(x_ref, h_ref, g_ref, b_ref, *refs, with_bf16, with_gate):
    refs = list(refs)
    gate_w_ref = refs.pop(0) if with_gate else None
    o_ref = refs.pop(0)
    t = ALPHA * x_ref[...] + h_ref[...].astype(F32)
    mu = jnp.mean(t, axis=-1, keepdims=True)
    d = t - mu
    var = jnp.mean(d * d, axis=-1, keepdims=True)
    r = d * lax.rsqrt(var + LN_EPS) * g_ref[...] + b_ref[...]
    o_ref[...] = r
    if with_bf16:
        rb = r.astype(BF16)
        refs.pop(0)[...] = rb
        if with_gate:
            refs.pop(0)[...] = _dot(rb, gate_w_ref[...])


def _ln_residual(x, h, g, b, *, tm=256, with_bf16=True, next_gate_w=None, name):
    m, d = x.shape
    tm = min(tm, m)
    row = pl.BlockSpec((tm, d), lambda i: (i, 0))
    vec = pl.BlockSpec((1, d), lambda i: (0, 0))
    with_gate = next_gate_w is not None
    assert with_bf16 or not with_gate
    in_specs, args = [row, row, vec, vec], [x, h, g.reshape(1, d), b.reshape(1, d)]
    out_specs, out_shape = [row], [jax.ShapeDtypeStruct((m, d), F32)]
    if with_gate:
        in_specs.append(pl.BlockSpec((d, LANES), lambda i: (0, 0)))
        args.append(next_gate_w)
    if with_bf16:
        out_specs.append(row)
        out_shape.append(jax.ShapeDtypeStruct((m, d), BF16))
    if with_gate:
        out_specs.append(pl.BlockSpec((tm, LANES), lambda i: (i, 0)))
        out_shape.append(jax.ShapeDtypeStruct((m, LANES), F32))
    out = list(pl.pallas_call(
        functools.partial(_ln_kernel, with_bf16=with_bf16, with_gate=with_gate),
        grid=(m // tm,),
        in_specs=in_specs,
        out_specs=out_specs,
        out_shape=out_shape,
        compiler_params=pltpu.CompilerParams(
            dimension_semantics=("parallel",), vmem_limit_bytes=VMEM_LIMIT),
        name=name,
    )(*args))
    return out[0], (out[1] if with_bf16 else None), (out[2] if with_gate else None)


def _cast_gate_kernel(x_ref, gate_w_ref, xb_ref, yg_ref):
    xb = x_ref[...].astype(BF16)
    xb_ref[...] = xb
    yg_ref[...] = _dot(xb, gate_w_ref[...])


def _cast_and_gate(x, gate_w, *, tm=512, name):
    m, d = x.shape
    tm = min(tm, m)
    row = pl.BlockSpec((tm, d), lambda i: (i, 0))
    return pl.pallas_call(
        _cast_gate_kernel,
        grid=(m // tm,),
        in_specs=[row, pl.BlockSpec((d, LANES), lambda i: (0, 0))],
        out_specs=[row, pl.BlockSpec((tm, LANES), lambda i: (i, 0))],
        out_shape=[jax.ShapeDtypeStruct((m, d), BF16), jax.ShapeDtypeStruct((m, LANES), F32)],
        compiler_params=pltpu.CompilerParams(
            dimension_semantics=("parallel",), vmem_limit_bytes=VMEM_LIMIT),
        name=name,
    )(x, gate_w)


def _stack_masks(stack):
    rows = stack * CHUNK
    srow = lax.broadcasted_iota(jnp.int32, (rows, rows), 0)
    scol = lax.broadcasted_iota(jnp.int32, (rows, rows), 1)
    same_head = (srow // CHUNK) == (scol // CHUNK)
    causal = jnp.logical_and(same_head, srow >= scol)
    strict = jnp.logical_and(same_head, srow > scol)
    pick = (lax.broadcasted_iota(jnp.int32, (stack, rows), 1) // CHUNK
            == lax.broadcasted_iota(jnp.int32, (stack, rows), 0))
    return srow, scol, causal, strict, pick


def _head_groups(n_heads, stack):
    return [range(g * stack, (g + 1) * stack) for g in range(n_heads // stack)]


def _stack(fn, heads):
    return jnp.concatenate([fn(h) for h in heads], axis=0)


def _stack_bcast(fn, heads):
    return jnp.concatenate([jnp.broadcast_to(fn(h), (CHUNK, fn(h).shape[1])) for h in heads], axis=0)


def _row_form(t_rows, pick):
    return jnp.sum(jnp.where(pick, jnp.concatenate([t_rows] * t_rows.shape[0], axis=1), 0.0),
                   axis=0, keepdims=True)


def _head_rows(j):
    return slice(j * CHUNK, (j + 1) * CHUNK)


def _chunk_rows(ci):
    return slice(ci * CHUNK, (ci + 1) * CHUNK)


def _mixer_call(kernel, y_specs, extra_specs, scratch, args, b, t, tb, name):
    return pl.pallas_call(
        kernel,
        grid=(b, t // tb),
        in_specs=y_specs + extra_specs,
        out_specs=pl.BlockSpec((1, tb, GROUP_WIDTH), lambda i, c: (i, c, 0)),
        out_shape=jax.ShapeDtypeStruct((b, t, GROUP_WIDTH), BF16),
        scratch_shapes=scratch,
        compiler_params=pltpu.CompilerParams(
            dimension_semantics=("arbitrary", "arbitrary"), vmem_limit_bytes=VMEM_LIMIT),
        name=name,
    )(*args)


def _tok_spec(tb, width, blk):
    return pl.BlockSpec((1, tb, width), lambda i, c: (i, c, blk))


def _full_spec(shape):
    return pl.BlockSpec(shape, lambda i, c: (0,) * len(shape))


def _gdn_parts(qkv_ref, z_ref, gate_ref, convw_ref, alog_ref, dtb_ref, normw_ref, o_ref,
               cbuf, cs_ref, s_ref, *, tb):
    hist = 8

    @pl.when(pl.program_id(1) == 0)
    def _():
        cbuf[0:hist, :] = jnp.zeros((hist, GDN_QKV), F32)
        s_ref[...] = jnp.zeros_like(s_ref)

    cbuf[hist:hist + tb, :] = qkv_ref[0]
    for off in range(0, GDN_QKV, LANES):
        acc = cbuf[hist:hist + tb, off:off + LANES] * convw_ref[CONV_K - 1:CONV_K, off:off + LANES]
        for s in range(1, CONV_K):
            acc = acc + (cbuf[hist - s:hist - s + tb, off:off + LANES]
                         * convw_ref[CONV_K - 1 - s:CONV_K - s, off:off + LANES])
        cs_ref[:, off:off + LANES] = acc * _sigmoid(acc)
    cbuf[0:hist, :] = cbuf[tb:tb + hist, :]

    groups = _head_groups(GDN_HEADS, GDN_STACK)

    row, col = _chunk_masks()
    srow, scol, causal, strict, pick = _stack_masks(GDN_STACK)

    def intra(ci):
        tok = _chunk_rows(ci)
        gates = gate_ref[0, tok, :]
        g_all = -jnp.exp(alog_ref[...]) * _softplus(gates + dtb_ref[...])
        beta_all = _sigmoid(gates)
        gam_all = _cumsum_rows(g_all, row, col)
        gam_t = gam_all.T

        qs, ks, vs, gam_cs, beta_cs, decays, a_mats = [], [], [], [], [], [], []
        for heads in groups:
            q = _stack(lambda h: cs_ref[tok, h * GDN_D:(h + 1) * GDN_D], heads)
            k = _stack(lambda h: cs_ref[tok, GROUP_WIDTH + h * GDN_D:GROUP_WIDTH + (h + 1) * GDN_D], heads)
            v = _stack(lambda h: cs_ref[tok, 2 * GROUP_WIDTH + h * GDN_D:2 * GROUP_WIDTH + (h + 1) * GDN_D], heads)
            q = q * (lax.rsqrt(jnp.sum(q * q, axis=-1, keepdims=True) + EPS) * GDN_D ** -0.5)
            k = k * lax.rsqrt(jnp.sum(k * k, axis=-1, keepdims=True) + EPS)
            gam_c = _stack(lambda h: gam_all[:, GA_OFF + h:GA_OFF + h + 1], heads)
            beta_c = _stack(lambda h: beta_all[:, GB_OFF + h:GB_OFF + h + 1], heads)
            gam_r = _row_form(gam_t[GA_OFF + heads[0]:GA_OFF + heads[0] + GDN_STACK, :], pick)
            decay = jnp.where(causal, jnp.exp(jnp.where(causal, gam_c - gam_r, 0.0)), 0.0)
            a_mats.append(jnp.where(strict, _dot_nt(k, k) * decay * beta_c, 0.0))
            qs.append(q); ks.append(k); vs.append(v)
            gam_cs.append(gam_c); beta_cs.append(beta_c); decays.append(decay)

        invs = _unit_lower_inverses(a_mats, srow, scol)
        out = []
        for g, heads in enumerate(groups):
            q, k, v, gam_c, beta_c = qs[g], ks[g], vs[g], gam_cs[g], beta_cs[g]
            egam = jnp.exp(gam_c)
            rhs = jnp.concatenate([v * beta_c, k * (beta_c * egam)], axis=1)
            sol = _dot(invs[g], rhs)
            qk = _dot_nt(q, k) * decays[g]
            gam_last = _stack_bcast(lambda h: gam_all[CHUNK - 1:CHUNK, GA_OFF + h:GA_OFF + h + 1], heads)
            k_end = k * jnp.exp(gam_last - gam_c)
            c_decs = [jnp.exp(gam_all[CHUNK - 1:CHUNK, GA_OFF + h:GA_OFF + h + 1]) for h in heads]
            out.append((sol, qk, q * egam, k_end, c_decs))
        return out

    def state_step(ci, pre):
        tok = _chunk_rows(ci)
        for g, heads in enumerate(groups):
            sol, qk, q_dec, k_end, c_decs = pre[g]
            v_new, q_state, states = [], [], []
            for j, h in enumerate(heads):
                sl = _head_rows(j)
                state = s_ref[h]
                both = _dot(jnp.concatenate([sol[sl, GDN_D:], q_dec[sl]], axis=0), state)
                v_new.append(sol[sl, :GDN_D] - both[:CHUNK])
                q_state.append(both[CHUNK:])
                states.append(state)
            o = jnp.concatenate(q_state, axis=0) + _dot(qk, jnp.concatenate(v_new, axis=0))
            for j, h in enumerate(heads):
                s_ref[h] = states[j] * c_decs[j] + _dot_tn(k_end[_head_rows(j)], v_new[j])
            o = o * lax.rsqrt(jnp.mean(o * o, axis=-1, keepdims=True) + EPS) * normw_ref[...]
            for j, h in enumerate(heads):
                zz = z_ref[0, tok, h * GDN_D:(h + 1) * GDN_D]
                o_ref[0, tok, h * GDN_D:(h + 1) * GDN_D] = (
                    o[_head_rows(j)] * (zz * _sigmoid(zz))).astype(o_ref.dtype)

    return intra, state_step


def _mlstm_parts(q_ref, k_ref, v_ref, og_ref, gate_ref, gb_ref, normw_ref, o_ref, c_ref, n_ref, m_ref):
    @pl.when(pl.program_id(1) == 0)
    def _():
        c_ref[...] = jnp.zeros_like(c_ref)
        n_ref[...] = jnp.zeros_like(n_ref)
        m_ref[...] = jnp.zeros_like(m_ref)

    groups = _head_groups(ML_HEADS, ML_STACK)
    row, col = _chunk_masks()
    _, _, causal, _, pick = _stack_masks(ML_STACK)

    m_sts = [_stack_bcast(lambda h: m_ref[h, 0:1, 0:1], heads) for heads in groups]

    def weights(ci):
        tok = _chunk_rows(ci)
        pre = gate_ref[0, tok, :] + gb_ref[...]
        fc_all = _cumsum_rows(_log_sigmoid(pre), row, col)
        fc_t = fc_all.T
        pre_t = pre.T
        per_group = []
        for g, heads in enumerate(groups):
            h0 = heads[0]
            q = _stack(lambda h: q_ref[0, tok, h * ML_DK:(h + 1) * ML_DK], heads)
            k = _stack(lambda h: k_ref[0, tok, h * ML_DK:(h + 1) * ML_DK], heads) * ML_DK ** -0.5
            v = _stack(lambda h: v_ref[0, tok, h * ML_DV:(h + 1) * ML_DV], heads)
            fc_c = _stack(lambda h: fc_all[:, MF_OFF + h:MF_OFF + h + 1], heads)
            ic_c = _stack(lambda h: pre[:, MI_OFF + h:MI_OFF + h + 1], heads)
            fc_last = _stack_bcast(lambda h: fc_all[CHUNK - 1:CHUNK, MF_OFF + h:MF_OFF + h + 1], heads)
            fc_r = _row_form(fc_t[MF_OFF + h0:MF_OFF + h0 + ML_STACK, :], pick)
            ic_r = _row_form(pre_t[MI_OFF + h0:MI_OFF + h0 + ML_STACK, :], pick)
            d_log = jnp.where(causal, fc_c - fc_r + ic_r, -jnp.inf)
            m_intra = jnp.max(d_log, axis=-1, keepdims=True)
            qk = _dot_nt(q, k)
            src_end = fc_last - fc_c + ic_c
            m_src = jnp.concatenate(
                [jnp.broadcast_to(jnp.max(src_end[_head_rows(j)], axis=0, keepdims=True), (CHUNK, 1))
                 for j in range(ML_STACK)], axis=0)
            m_st = m_sts[g]
            m_inter = fc_c + m_st
            m_t = jnp.maximum(m_inter, m_intra)
            w_inter = jnp.exp(m_inter - m_t)
            w_intra = jnp.exp(d_log - m_t) * qk
            m_new = jnp.maximum(fc_last + m_st, m_src)
            w_state = jnp.exp(fc_last + m_st - m_new)
            kw = k * jnp.exp(src_end - m_new)
            m_sts[g] = m_new
            per_group.append((q, w_inter, _dot(w_intra, v), jnp.sum(w_intra, axis=-1, keepdims=True),
                              jnp.exp(-m_t), w_state,
                              [_dot_tn(kw[_head_rows(j)], v[_head_rows(j)]) for j in range(ML_STACK)],
                              [jnp.sum(kw[_head_rows(j)], axis=0, keepdims=True) for j in range(ML_STACK)]))
        return per_group

    c_sts = [[c_ref[h] for h in heads] for heads in groups]
    n_sts = [[n_ref[h, 0:1, :] for h in heads] for heads in groups]

    def recur(ci, cur):
        tok = _chunk_rows(ci)
        for g, heads in enumerate(groups):
            q, w_inter, intra_num, intra_den, floor, w_state, kv, ksum = cur[g]
            q_c = jnp.concatenate([_dot(q[_head_rows(j)], c_sts[g][j]) for j in range(ML_STACK)], axis=0)
            n_rows = jnp.concatenate(
                [jnp.broadcast_to(n_sts[g][j], (CHUNK, ML_DK)) for j in range(ML_STACK)], axis=0)
            num = w_inter * q_c + intra_num
            den = w_inter * jnp.sum(q * n_rows, axis=-1, keepdims=True) + intra_den
            hh = num / jnp.maximum(jnp.abs(den), floor)
            hh = hh * lax.rsqrt(jnp.mean(hh * hh, axis=-1, keepdims=True) + EPS) * normw_ref[...]
            for j, h in enumerate(heads):
                gg = og_ref[0, tok, h * ML_DV:(h + 1) * ML_DV]
                o_ref[0, tok, h * ML_DV:(h + 1) * ML_DV] = (hh[_head_rows(j)] * _sigmoid(gg)).astype(o_ref.dtype)
                ws = w_state[j * CHUNK:j * CHUNK + 1, :]
                c_sts[g][j] = ws * c_sts[g][j] + kv[j]
                n_sts[g][j] = ws * n_sts[g][j] + ksum[j]

    def finish():
        for g, heads in enumerate(groups):
            for j, h in enumerate(heads):
                c_ref[h] = c_sts[g][j]
                n_ref[h] = jnp.broadcast_to(n_sts[g][j], (8, ML_DK))
                m_ref[h] = jnp.broadcast_to(m_sts[g][j * CHUNK:j * CHUNK + 1, :], (8, LANES))

    return weights, recur, finish


def _pipelined_chunks(n_chunks, prepare, consume):
    cur = prepare(0)
    for ci in range(n_chunks):
        nxt = prepare(ci + 1) if ci + 1 < n_chunks else None
        consume(ci, cur)
        cur = nxt


def _gdn_kernel(*refs, tb):
    intra, state_step = _gdn_parts(*refs, tb=tb)
    _pipelined_chunks(tb // CHUNK, intra, state_step)


def _mlstm_kernel(*refs, tb):
    weights, recur, finish = _mlstm_parts(*refs)
    _pipelined_chunks(tb // CHUNK, weights, recur)
    finish()


def _gdn_mixer(y, yg, conv_w, alog_row, dtb_row, norm_w):
    b, t, _ = y.shape
    tb = min(GDN_BLOCK, t)
    return _mixer_call(
        functools.partial(_gdn_kernel, tb=tb),
        [_tok_spec(tb, GDN_QKV, 0), _tok_spec(tb, GROUP_WIDTH, GDN_QKV // GROUP_WIDTH), _tok_spec(tb, LANES, 0)],
        [_full_spec((CONV_K, GDN_QKV)), _full_spec((1, LANES)), _full_spec((1, LANES)), _full_spec((1, GDN_D))],
        [pltpu.VMEM((tb + 8, GDN_QKV), F32), pltpu.VMEM((tb, GDN_QKV), F32),
         pltpu.VMEM((GDN_HEADS, GDN_D, GDN_D), F32)],
        (y, y, yg, conv_w, alog_row, dtb_row, norm_w), b, t, tb, "gdn_mixer")


def _mlstm_mixer(y, yg, gb_row, norm_w):
    b, t, _ = y.shape
    tb = min(MIX_BLOCK, t)
    qk_w = ML_HEADS * ML_DK
    base = GDN_QKV + GROUP_WIDTH
    vblk = (base + 2 * qk_w) // GROUP_WIDTH
    return _mixer_call(
        functools.partial(_mlstm_kernel, tb=tb),
        [_tok_spec(tb, qk_w, base // qk_w), _tok_spec(tb, qk_w, base // qk_w + 1),
         _tok_spec(tb, GROUP_WIDTH, vblk), _tok_spec(tb, GROUP_WIDTH, vblk + 1), _tok_spec(tb, LANES, 0)],
        [_full_spec((1, LANES)), _full_spec((1, ML_DV))],
        [pltpu.VMEM((ML_HEADS, ML_DK, ML_DV), F32), pltpu.VMEM((ML_HEADS, 8, ML_DK), F32),
         pltpu.VMEM((ML_HEADS, 8, LANES), F32)],
        (y, y, y, y, yg, gb_row, norm_w), b, t, tb, "mlstm_mixer")


def _gla_intra(q, k, v, bcum, bcum_last, stack, causal):
    q_dec = q * jnp.exp(bcum)
    k_inv = k * jnp.exp(-bcum)
    attn = jnp.where(causal, _dot_nt(q_dec, k_inv), 0.0)
    o_intra = _dot(attn, v)
    k_end = k * jnp.exp(bcum_last - bcum)
    kv = [_dot_tn(k_end[_head_rows(j)], v[_head_rows(j)]) for j in range(stack)]
    return q_dec, o_intra, kv


def _gla_recur(q_dec, o_intra, kv, end_decay_cols, states):
    o = o_intra + jnp.concatenate(
        [_dot(q_dec[_head_rows(j)], states[j]) for j in range(len(states))], axis=0)
    return o, [states[j] * end_decay_cols[j] + kv[j] for j in range(len(states))]


def _ret_kernel(q_ref, k_ref, v_ref, g_ref, cos_ref, sin_ref, o_ref, s_ref, *, tb):
    @pl.when(pl.program_id(1) == 0)
    def _():
        s_ref[...] = jnp.zeros_like(s_ref)

    groups = _head_groups(RET_HEADS, RET_STACK)

    _, _, causal, _, _ = _stack_masks(RET_STACK)
    steps = (lax.broadcasted_iota(jnp.int32, (CHUNK, 1), 0) + 1).astype(F32)

    def intra(ci, heads):
        tok = _chunk_rows(ci)
        cos2 = jnp.concatenate([cos_ref[0, tok, :]] * RET_STACK, axis=0)
        sin2 = jnp.concatenate([sin_ref[0, tok, :]] * RET_STACK, axis=0)
        log_gamma = [math.log(1.0 - 2.0 ** (-5.0 - h)) for h in heads]
        q = _stack(lambda h: q_ref[0, tok, h * RET_DK:(h + 1) * RET_DK], heads)
        k = _stack(lambda h: k_ref[0, tok, h * RET_DK:(h + 1) * RET_DK], heads)
        v = _stack(lambda h: v_ref[0, tok, h * RET_DV:(h + 1) * RET_DV], heads)
        q = q * cos2 + pltpu.roll(q, RET_DK // 2, 1) * sin2
        k = (k * cos2 + pltpu.roll(k, RET_DK // 2, 1) * sin2) * RET_DK ** -0.5
        bcum = jnp.concatenate([steps * lg for lg in log_gamma], axis=0)
        bcum_last = jnp.concatenate([jnp.full((CHUNK, 1), CHUNK * lg, F32) for lg in log_gamma], axis=0)
        return _gla_intra(q, k, v, bcum, bcum_last, RET_STACK, causal)

    n_chunks = tb // CHUNK
    pre = [[intra(ci, heads) for heads in groups] for ci in range(n_chunks)]
    for g, heads in enumerate(groups):
        ends = [jnp.exp(jnp.full((1, 1), CHUNK * math.log(1.0 - 2.0 ** (-5.0 - h)), F32)) for h in heads]
        states = [s_ref[h] for h in heads]
        for ci in range(n_chunks):
            tok = _chunk_rows(ci)
            o, states = _gla_recur(*pre[ci][g], ends, states)
            o = o * lax.rsqrt(jnp.mean(o * o, axis=-1, keepdims=True) + EPS)
            for j, h in enumerate(heads):
                gg = g_ref[0, tok, h * RET_DV:(h + 1) * RET_DV]
                o_ref[0, tok, h * RET_DV:(h + 1) * RET_DV] = (
                    o[_head_rows(j)] * (gg * _sigmoid(gg))).astype(o_ref.dtype)
        for j, h in enumerate(heads):
            s_ref[h] = states[j]


def _ret_mixer(y, cos2, sin2):
    b, t, _ = y.shape
    tb = min(MIX_BLOCK, t)
    qk_w = RET_HEADS * RET_DK
    return _mixer_call(
        functools.partial(_ret_kernel, tb=tb),
        [_tok_spec(tb, qk_w, 0), _tok_spec(tb, qk_w, 1), _tok_spec(tb, GROUP_WIDTH, 1),
         _tok_spec(tb, GROUP_WIDTH, 2), _tok_spec(tb, RET_DK, 0), _tok_spec(tb, RET_DK, 0)],
        [],
        [pltpu.VMEM((RET_HEADS, RET_DK, RET_DV), F32)],
        (y, y, y, y, cos2, sin2), b, t, tb, "ret_mixer")


def _gla_kernel(q_ref, k_ref, v_ref, r_ref, la_ref, wup_ref, bup_ref, normw_ref, o_ref, s_ref, *, tb):
    @pl.when(pl.program_id(1) == 0)
    def _():
        s_ref[...] = jnp.zeros_like(s_ref)

    groups = _head_groups(GLA_HEADS, GLA_STACK)

    row, col = _chunk_masks()
    _, _, causal, _, _ = _stack_masks(GLA_STACK)

    for ci in range(tb // CHUNK):
        tok = _chunk_rows(ci)
        gate = _dot(la_ref[0, tok, :], wup_ref[...]) + bup_ref[...]
        bcum_all = _cumsum_rows(_log_sigmoid(gate) * (1.0 / GLA_TAU), row, col)
        for heads in groups:
            q = _stack(lambda h: q_ref[0, tok, h * GLA_DK:(h + 1) * GLA_DK], heads) * GLA_DK ** -0.5
            k = _stack(lambda h: k_ref[0, tok, h * GLA_DK:(h + 1) * GLA_DK], heads)
            v = _stack(lambda h: v_ref[0, tok, h * GLA_DV:(h + 1) * GLA_DV], heads)
            bcum = _stack(lambda h: bcum_all[:, h * GLA_DK:(h + 1) * GLA_DK], heads)
            bcum_last = _stack_bcast(lambda h: bcum_all[CHUNK - 1:CHUNK, h * GLA_DK:(h + 1) * GLA_DK], heads)
            ends = [jnp.exp(bcum_all[:, h * GLA_DK:(h + 1) * GLA_DK].T[:, CHUNK - 1:CHUNK]) for h in heads]
            q_dec = q * jnp.exp(bcum)
            k_inv = k * jnp.exp(-bcum)
            attn = jnp.where(causal, _dot_nt(q_dec, k_inv), 0.0)
            states = [s_ref[h] for h in heads]
            o = _dot(attn, v) + jnp.concatenate(
                [_dot(q_dec[_head_rows(j)], states[j]) for j in range(GLA_STACK)], axis=0)
            k_end = k * jnp.exp(bcum_last - bcum)
            for j, h in enumerate(heads):
                s_ref[h] = states[j] * ends[j] + _dot_tn(k_end[_head_rows(j)], v[_head_rows(j)])
            o = o * lax.rsqrt(jnp.mean(o * o, axis=-1, keepdims=True) + EPS) * normw_ref[...]
            for j, h in enumerate(heads):
                rr = r_ref[0, tok, h * GLA_DV:(h + 1) * GLA_DV]
                o_ref[0, tok, h * GLA_DV:(h + 1) * GLA_DV] = (
                    o[_head_rows(j)] * (rr * _sigmoid(rr))).astype(o_ref.dtype)


def _gla_mixer(y, yla, w_up_pad, b_up, norm_w):
    b, t, _ = y.shape
    tb = min(GLA_BLOCK, t)
    qk_w = GLA_HEADS * GLA_DK
    base = 2 * RET_HEADS * RET_DK + 2 * GROUP_WIDTH
    vblk = (base + 2 * qk_w) // GROUP_WIDTH
    return _mixer_call(
        functools.partial(_gla_kernel, tb=tb),
        [_tok_spec(tb, qk_w, base // qk_w), _tok_spec(tb, qk_w, base // qk_w + 1),
         _tok_spec(tb, GROUP_WIDTH, vblk), _tok_spec(tb, GROUP_WIDTH, vblk + 1), _tok_spec(tb, LANES, 0)],
        [_full_spec((LANES, qk_w)), _full_spec((1, qk_w)), _full_spec((1, GLA_DV))],
        [pltpu.VMEM((GLA_HEADS, GLA_DK, GLA_DV), F32)],
        (y, y, y, y, yla, w_up_pad, b_up, norm_w), b, t, tb, "gla_mixer")


def _pad_lanes(v, offset=0):
    return jnp.zeros((1, LANES), F32).at[0, offset:offset + v.shape[0]].set(v.astype(F32))


_EVEN_G_END = GDN_QKV + GROUP_WIDTH
_EVEN_M_START = _EVEN_G_END + 2 * GDN_HEADS
_EVEN_M_END = _EVEN_M_START + EVEN_MAIN - _EVEN_G_END


def _gate_weight(layer, e_w_in, o_w_in):
    j = layer // 2
    if layer % 2 == 0:
        w_in = e_w_in[j]
        w_gate = jnp.concatenate([w_in[:, _EVEN_G_END:_EVEN_M_START], w_in[:, _EVEN_M_END:]], axis=1)
    else:
        w_gate = o_w_in[j][:, ODD_MAIN:]
    return jnp.pad(w_gate, ((0, 0), (0, LANES - w_gate.shape[1]))).astype(BF16)


def _even_mixer(xb, yg2, w_in_all, j, conv_w, a_log, dt_bias, gdn_norm_w, ml_gate_b, ml_norm_w, w_out_all):
    b, t, d = xb.shape
    x2 = xb.reshape(b * t, d)
    w_in = w_in_all[j]
    w_main = jnp.concatenate([w_in[:, :_EVEN_G_END], w_in[:, _EVEN_M_START:_EVEN_M_END]], axis=1).astype(BF16)
    w_out = _cast_bf16(w_out_all, j, "even_w_out_cast")
    y = _matmul(x2, w_main, out_dtype=F32, tm=1024, tn=1024, tk=d, name="even_in_proj").reshape(b, t, EVEN_MAIN)
    yg = yg2.reshape(b, t, LANES)
    o_a = _gdn_mixer(y, yg, conv_w.astype(F32), _pad_lanes(a_log, GA_OFF), _pad_lanes(dt_bias, GA_OFF),
                     gdn_norm_w.reshape(1, GDN_D).astype(F32))
    o_b = _mlstm_mixer(y, yg, _pad_lanes(ml_gate_b, MI_OFF), ml_norm_w.reshape(1, ML_DV).astype(F32))
    return _matmul_pair(o_a.reshape(b * t, GROUP_WIDTH), o_b.reshape(b * t, GROUP_WIDTH), w_out,
                        out_dtype=BRANCH_DTYPE, tm=1024, tn=1024, name="even_out_proj")


def _odd_mixer(xb, yg2, positions, w_in_all, j, gla_w_up, gla_b_up, gla_norm_w, w_out_all):
    b, t, d = xb.shape
    x2 = xb.reshape(b * t, d)
    w_main = w_in_all[j][:, :ODD_MAIN].astype(BF16)
    w_out = _cast_bf16(w_out_all, j, "odd_w_out_cast")
    y = _matmul(x2, w_main, out_dtype=F32, tm=1024, tn=1024, tk=d, name="odd_in_proj").reshape(b, t, ODD_MAIN)
    yla = yg2.reshape(b, t, LANES)
    inv_freq = 1.0 / (ROPE_BASE ** jnp.linspace(0.0, 1.0, RET_DK // 2, dtype=F32))
    theta = positions.astype(F32)[:, :, None] * inv_freq
    cos, sin = jnp.cos(theta), jnp.sin(theta)
    cos2 = jnp.concatenate([cos, cos], axis=-1)
    sin2 = jnp.concatenate([-sin, sin], axis=-1)
    o_c = _ret_mixer(y, cos2, sin2)
    w_up_pad = jnp.pad(gla_w_up.astype(F32), ((0, LANES - GLA_RANK), (0, 0)))
    o_d = _gla_mixer(y, yla, w_up_pad, gla_b_up.reshape(1, -1).astype(F32),
                     gla_norm_w.reshape(1, GLA_DV).astype(F32))
    return _matmul_pair(o_c.reshape(b * t, GROUP_WIDTH), o_d.reshape(b * t, GROUP_WIDTH), w_out,
                        out_dtype=BRANCH_DTYPE, tm=1024, tn=1024, name="odd_out_proj")


def _mlp(xb2, w_up_all, w_down_all, layer):
    w_up = _cast_bf16(w_up_all, layer, f"mlp_w_up_cast_{layer}")
    w_down = _cast_bf16(w_down_all, layer, f"mlp_w_down_cast_{layer}")
    hdn = _matmul(xb2, w_up, out_dtype=BF16, tm=1024, tn=1024, tk=D_MODEL, relu2=True, name=f"mlp_up_{layer}")
    return _matmul(hdn, w_down, out_dtype=BRANCH_DTYPE, tm=1024, tn=1024, tk=4096, name=f"mlp_down_{layer}")


def kernel(x, positions, e_w_in, e_conv_w, e_a_log, e_dt_bias, e_gdn_norm_w, e_mlstm_gate_b, e_mlstm_norm_w, e_w_out, o_w_in, o_gla_w_up, o_gla_b_up, o_gla_norm_w, o_w_out, ln_mix_g, ln_mix_b, mlp_w_up, mlp_w_down, ln_mlp_g, ln_mlp_b):
    b, t, d = x.shape
    x2 = x.reshape(b * t, d)
    xb2, yg2 = _cast_and_gate(x2, _gate_weight(0, e_w_in, o_w_in), name="input_cast_gate")
    for layer in range(DEPTH):
        j = layer // 2
        xb = xb2.reshape(b, t, d)
        if layer % 2 == 0:
            h = _even_mixer(xb, yg2, e_w_in, j, e_conv_w[j], e_a_log[j], e_dt_bias[j], e_gdn_norm_w[j],
                            e_mlstm_gate_b[j], e_mlstm_norm_w[j], e_w_out)
        else:
            h = _odd_mixer(xb, yg2, positions, o_w_in, j, o_gla_w_up[j], o_gla_b_up[j], o_gla_norm_w[j], o_w_out)
        x2, xb2, _ = _ln_residual(x2, h, ln_mix_g[layer], ln_mix_b[layer], name=f"ln_mix_{layer}")
        f = _mlp(xb2, mlp_w_up, mlp_w_down, layer)
        more = layer + 1 < DEPTH
        x2, xb2, yg2 = _ln_residual(x2, f, ln_mlp_g[layer], ln_mlp_b[layer], with_bf16=more,
                                    next_gate_w=_gate_weight(layer + 1, e_w_in, o_w_in) if more else None,
                                    name=f"ln_mlp_{layer}")
    return x2.reshape(b, t, d)
```

```python
import functools
import math

import jax
import jax.numpy as jnp
from jax import lax
from jax.experimental import pallas as pl
from jax.experimental.pallas import tpu as pltpu

F32 = jnp.float32
BF16 = jnp.bfloat16
HIGHEST = lax.Precision.HIGHEST

D_MODEL = 4096
DEPTH = 2
CHUNK = 64
D_FF = 4 * D_MODEL
GROUP_WIDTH = D_MODEL // 2
ALPHA = (2.0 * DEPTH) ** 0.25
EPS = 1e-6
LN_EPS = 1e-5

GDN_HEADS = 16
GDN_D = GROUP_WIDTH // GDN_HEADS
CONV_K = 4
GDN_QKV = 3 * GROUP_WIDTH
ML_HEADS = 8
ML_DV = GROUP_WIDTH // ML_HEADS
ML_DK = ML_DV // 2
RET_HEADS = 8
RET_DV = GROUP_WIDTH // RET_HEADS
RET_DK = RET_DV // 2
ROPE_BASE = 10000.0
GLA_HEADS = 4
GLA_DV = GROUP_WIDTH // GLA_HEADS
GLA_DK = GLA_DV // 2
GLA_RANK = 16
GLA_TAU = 16.0

GDN_STACK, ML_STACK, RET_STACK, GLA_STACK = 2, 4, 2, 4
GDN_BLOCK = 4 * CHUNK
MIX_BLOCK = 8 * CHUNK
GLA_BLOCK = 4 * CHUNK
LANES = 128
EVEN_MAIN = GDN_QKV + GROUP_WIDTH + 2 * ML_HEADS * ML_DK + 2 * GROUP_WIDTH
ODD_MAIN = 2 * RET_HEADS * RET_DK + 2 * GROUP_WIDTH + 2 * GLA_HEADS * GLA_DK + 2 * GROUP_WIDTH
GA_OFF, GB_OFF, MI_OFF, MF_OFF = 0, GDN_HEADS, 2 * GDN_HEADS, 2 * GDN_HEADS + ML_HEADS

BRANCH_DTYPE = BF16
VMEM_LIMIT = 56 * 1024 * 1024
CAST_BLOCK_BYTES = 8 * 1024 * 1024


def _sigmoid(x):
    return 1.0 / (1.0 + jnp.exp(-x))


def _softplus(x):
    return jnp.maximum(x, 0.0) + jnp.log1p(jnp.exp(-jnp.abs(x)))


def _log_sigmoid(x):
    return -_softplus(-x)


def _mxu(x):
    return x.astype(BF16)


def _dot(a, b):
    return jnp.dot(_mxu(a), _mxu(b), preferred_element_type=F32)


def _dot_nt(a, b):
    return lax.dot_general(_mxu(a), _mxu(b), (((1,), (1,)), ((), ())), preferred_element_type=F32)


def _dot_tn(a, b):
    return _dot(a.T, b)


def _chunk_masks():
    row = lax.broadcasted_iota(jnp.int32, (CHUNK, CHUNK), 0)
    col = lax.broadcasted_iota(jnp.int32, (CHUNK, CHUNK), 1)
    return row, col


def _cumsum_rows(x, row, col):
    tril = jnp.where(row >= col, 1.0, 0.0).astype(F32)
    return jnp.dot(tril, x, precision=HIGHEST, preferred_element_type=F32)


def _unit_lower_inverses(mats, row, col):
    eye = jnp.where(row == col, 1.0, 0.0).astype(F32)
    same = (row >> 1) == (col >> 1)
    invs = [eye - jnp.where(same, a, 0.0) for a in mats]
    shift = 2
    while (1 << (shift - 1)) < CHUNK:
        same2 = (row >> shift) == (col >> shift)
        off_mask = jnp.logical_and(same2, jnp.logical_not(same))
        inv_b = [_mxu(inv) for inv in invs]
        tmp = [_dot(jnp.where(off_mask, a, 0.0), ib) for a, ib in zip(mats, inv_b)]
        invs = [inv - _dot(ib, t) for inv, ib, t in zip(invs, inv_b, tmp)]
        same = same2
        shift += 1
    return invs


def _mm_kernel_single(a_ref, b_ref, o_ref, *, relu2):
    r = _dot(a_ref[...], b_ref[...])
    if relu2:
        r = jnp.square(jnp.maximum(r, 0.0))
    o_ref[...] = r.astype(o_ref.dtype)


def _mm_kernel_acc(a_ref, b_ref, o_ref, acc_ref, *, nk, relu2):
    k = pl.program_id(2)

    @pl.when(k == 0)
    def _():
        acc_ref[...] = jnp.zeros_like(acc_ref)

    acc_ref[...] += _dot(a_ref[...], b_ref[...])

    @pl.when(k == nk - 1)
    def _():
        r = acc_ref[...]
        if relu2:
            r = jnp.square(jnp.maximum(r, 0.0))
        o_ref[...] = r.astype(o_ref.dtype)


def _matmul(a, b, *, out_dtype, tm, tn, tk, relu2=False, n=None, name):
    m, k = a.shape
    n = b.shape[1] if n is None else n
    tm, tn, tk = min(tm, m), min(tn, n), min(tk, k)
    assert m % tm == 0 and n % tn == 0 and k % tk == 0
    nk = k // tk
    if nk == 1:
        return pl.pallas_call(
            functools.partial(_mm_kernel_single, relu2=relu2),
            grid=(m // tm, n // tn),
            in_specs=[pl.BlockSpec((tm, k), lambda i, j: (i, 0)),
                      pl.BlockSpec((k, tn), lambda i, j: (0, j))],
            out_specs=pl.BlockSpec((tm, tn), lambda i, j: (i, j)),
            out_shape=jax.ShapeDtypeStruct((m, n), out_dtype),
            compiler_params=pltpu.CompilerParams(
                dimension_semantics=("parallel", "parallel"), vmem_limit_bytes=VMEM_LIMIT),
            name=name,
        )(a, b)
    return pl.pallas_call(
        functools.partial(_mm_kernel_acc, nk=nk, relu2=relu2),
        grid=(m // tm, n // tn, nk),
        in_specs=[pl.BlockSpec((tm, tk), lambda i, j, kk: (i, kk)),
                  pl.BlockSpec((tk, tn), lambda i, j, kk: (kk, j))],
        out_specs=pl.BlockSpec((tm, tn), lambda i, j, kk: (i, j)),
        out_shape=jax.ShapeDtypeStruct((m, n), out_dtype),
        scratch_shapes=[pltpu.VMEM((tm, tn), F32)],
        compiler_params=pltpu.CompilerParams(
            dimension_semantics=("parallel", "parallel", "arbitrary"), vmem_limit_bytes=VMEM_LIMIT),
        name=name,
    )(a, b)


def _mm_pair_kernel(a1_ref, a2_ref, b_ref, o_ref):
    k1 = a1_ref.shape[1]
    r = _dot(a1_ref[...], b_ref[0:k1, :]) + _dot(a2_ref[...], b_ref[k1:, :])
    o_ref[...] = r.astype(o_ref.dtype)


def _matmul_pair(a1, a2, b, *, out_dtype, tm, tn, name):
    m, k1 = a1.shape
    k2 = a2.shape[1]
    n = b.shape[1]
    tm, tn = min(tm, m), min(tn, n)
    assert m % tm == 0 and n % tn == 0 and b.shape[0] == k1 + k2
    return pl.pallas_call(
        _mm_pair_kernel,
        grid=(m // tm, n // tn),
        in_specs=[pl.BlockSpec((tm, k1), lambda i, j: (i, 0)),
                  pl.BlockSpec((tm, k2), lambda i, j: (i, 0)),
                  pl.BlockSpec((k1 + k2, tn), lambda i, j: (0, j))],
        out_specs=pl.BlockSpec((tm, tn), lambda i, j: (i, j)),
        out_shape=jax.ShapeDtypeStruct((m, n), out_dtype),
        compiler_params=pltpu.CompilerParams(
            dimension_semantics=("parallel", "parallel"), vmem_limit_bytes=VMEM_LIMIT),
        name=name,
    )(a1, a2, b)


def _cast_rows(rows, width):
    br = min(rows, max(8, (CAST_BLOCK_BYTES // (4 * width)) // 8 * 8))
    while rows % br:
        br -= 8
    return br


def _cast_kernel(w_ref, o_ref):
    o_ref[...] = w_ref[...].astype(o_ref.dtype)


def _cast_bf16(w, layer, name):
    _, r, c = w.shape
    br = _cast_rows(r, c)
    return pl.pallas_call(
        _cast_kernel,
        grid=(r // br,),
        in_specs=[pl.BlockSpec((None, br, c), lambda i: (layer, i, 0))],
        out_specs=pl.BlockSpec((br, c), lambda i: (i, 0)),
        out_shape=jax.ShapeDtypeStruct((r, c), BF16),
        compiler_params=pltpu.CompilerParams(dimension_semantics=("parallel",), vmem_limit_bytes=VMEM_LIMIT),
        name=name,
    )(w)


def _ln_kernel(x_ref, h_ref, g_ref, b_ref, *refs, with_bf16, with_gate):
    refs = list(refs)
    gate_w_ref = refs.pop(0) if with_gate else None
    o_ref = refs.pop(0)
    t = ALPHA * x_ref[...] + h_ref[...].astype(F32)
    mu = jnp.mean(t, axis=-1, keepdims=True)
    d = t - mu
    var = jnp.mean(d * d, axis=-1, keepdims=True)
    r = d * lax.rsqrt(var + LN_EPS) * g_ref[...] + b_ref[...]
    o_ref[...] = r
    if with_bf16:
        rb = r.astype(BF16)
        refs.pop(0)[...] = rb
        if with_gate:
            refs.pop(0)[...] = _dot(rb, gate_w_ref[...])


def _ln_residual(x, h, g, b, *, tm=256, with_bf16=True, next_gate_w=None, name):
    m, d = x.shape
    tm = min(tm, m)
    row = pl.BlockSpec((tm, d), lambda i: (i, 0))
    vec = pl.BlockSpec((1, d), lambda i: (0, 0))
    with_gate = next_gate_w is not None
    assert with_bf16 or not with_gate
    in_specs, args = [row, row, vec, vec], [x, h, g.reshape(1, d), b.reshape(1, d)]
    out_specs, out_shape = [row], [jax.ShapeDtypeStruct((m, d), F32)]
    if with_gate:
        gate_w, gate_blk = next_gate_w
        in_specs.append(pl.BlockSpec((d, LANES), lambda i: (0, gate_blk)))
        args.append(gate_w)
    if with_bf16:
        out_specs.append(row)
        out_shape.append(jax.ShapeDtypeStruct((m, d), BF16))
    if with_gate:
        out_specs.append(pl.BlockSpec((tm, LANES), lambda i: (i, 0)))
        out_shape.append(jax.ShapeDtypeStruct((m, LANES), F32))
    out = list(pl.pallas_call(
        functools.partial(_ln_kernel, with_bf16=with_bf16, with_gate=with_gate),
        grid=(m // tm,),
        in_specs=in_specs,
        out_specs=out_specs,
        out_shape=out_shape,
        compiler_params=pltpu.CompilerParams(
            dimension_semantics=("parallel",), vmem_limit_bytes=VMEM_LIMIT),
        name=name,
    )(*args))
    return out[0], (out[1] if with_bf16 else None), (out[2] if with_gate else None)


def _cast_gate_kernel(x_ref, gate_w_ref, xb_ref, yg_ref):
    xb = x_ref[...].astype(BF16)
    xb_ref[...] = xb
    yg_ref[...] = _dot(xb, gate_w_ref[...])


def _cast_and_gate(x, gate, *, tm=512, name):
    m, d = x.shape
    tm = min(tm, m)
    row = pl.BlockSpec((tm, d), lambda i: (i, 0))
    gate_w, gate_blk = gate
    return pl.pallas_call(
        _cast_gate_kernel,
        grid=(m // tm,),
        in_specs=[row, pl.BlockSpec((d, LANES), lambda i: (0, gate_blk))],
        out_specs=[row, pl.BlockSpec((tm, LANES), lambda i: (i, 0))],
        out_shape=[jax.ShapeDtypeStruct((m, d), BF16), jax.ShapeDtypeStruct((m, LANES), F32)],
        compiler_params=pltpu.CompilerParams(
            dimension_semantics=("parallel",), vmem_limit_bytes=VMEM_LIMIT),
        name=name,
    )(x, gate_w)


def _stack_masks(stack):
    rows = stack * CHUNK
    srow = lax.broadcasted_iota(jnp.int32, (rows, rows), 0)
    scol = lax.broadcasted_iota(jnp.int32, (rows, rows), 1)
    same_head = (srow // CHUNK) == (scol // CHUNK)
    causal = jnp.logical_and(same_head, srow >= scol)
    strict = jnp.logical_and(same_head, srow > scol)
    pick = (lax.broadcasted_iota(jnp.int32, (stack, rows), 1) // CHUNK
            == lax.broadcasted_iota(jnp.int32, (stack, rows), 0))
    return srow, scol, causal, strict, pick


def _head_groups(n_heads, stack):
    return [range(g * stack, (g + 1) * stack) for g in range(n_heads // stack)]


def _stack(fn, heads):
    return jnp.concatenate([fn(h) for h in heads], axis=0)


def _stack_bcast(fn, heads):
    return jnp.concatenate([jnp.broadcast_to(fn(h), (CHUNK, fn(h).shape[1])) for h in heads], axis=0)


def _row_form(t_rows, pick):
    return jnp.sum(jnp.where(pick, jnp.concatenate([t_rows] * t_rows.shape[0], axis=1), 0.0),
                   axis=0, keepdims=True)


def _head_rows(j):
    return slice(j * CHUNK, (j + 1) * CHUNK)


def _chunk_rows(ci):
    return slice(ci * CHUNK, (ci + 1) * CHUNK)


def _mixer_call(kernel, y_specs, extra_specs, scratch, args, b, t, tb, name):
    return pl.pallas_call(
        kernel,
        grid=(b, t // tb),
        in_specs=y_specs + extra_specs,
        out_specs=pl.BlockSpec((1, tb, GROUP_WIDTH), lambda i, c: (i, c, 0)),
        out_shape=jax.ShapeDtypeStruct((b, t, GROUP_WIDTH), BF16),
        scratch_shapes=scratch,
        compiler_params=pltpu.CompilerParams(
            dimension_semantics=("arbitrary", "arbitrary"), vmem_limit_bytes=VMEM_LIMIT),
        name=name,
    )(*args)


def _tok_spec(tb, width, blk):
    return pl.BlockSpec((1, tb, width), lambda i, c: (i, c, blk))


def _full_spec(shape):
    return pl.BlockSpec(shape, lambda i, c: (0,) * len(shape))


def _gdn_parts(qkv_ref, z_ref, gate_ref, convw_ref, alog_ref, dtb_ref, normw_ref, o_ref,
               cbuf, cs_ref, s_ref, *, tb):
    hist = 8

    @pl.when(pl.program_id(1) == 0)
    def _():
        cbuf[0:hist, :] = jnp.zeros((hist, GDN_QKV), F32)
        s_ref[...] = jnp.zeros_like(s_ref)

    cbuf[hist:hist + tb, :] = qkv_ref[0]
    for off in range(0, GDN_QKV, LANES):
        acc = cbuf[hist:hist + tb, off:off + LANES] * convw_ref[CONV_K - 1:CONV_K, off:off + LANES]
        for s in range(1, CONV_K):
            acc = acc + (cbuf[hist - s:hist - s + tb, off:off + LANES]
                         * convw_ref[CONV_K - 1 - s:CONV_K - s, off:off + LANES])
        cs_ref[:, off:off + LANES] = acc * _sigmoid(acc)
    cbuf[0:hist, :] = cbuf[tb:tb + hist, :]

    groups = _head_groups(GDN_HEADS, GDN_STACK)

    row, col = _chunk_masks()
    srow, scol, causal, strict, pick = _stack_masks(GDN_STACK)

    def intra(ci):
        tok = _chunk_rows(ci)
        gates = gate_ref[0, tok, :]
        g_all = -jnp.exp(alog_ref[...]) * _softplus(gates + dtb_ref[...])
        beta_all = _sigmoid(gates)
        gam_all = _cumsum_rows(g_all, row, col)
        gam_t = gam_all.T

        qs, ks, vs, gam_cs, beta_cs, decays, a_mats = [], [], [], [], [], [], []
        for heads in groups:
            q = _stack(lambda h: cs_ref[tok, h * GDN_D:(h + 1) * GDN_D], heads)
            k = _stack(lambda h: cs_ref[tok, GROUP_WIDTH + h * GDN_D:GROUP_WIDTH + (h + 1) * GDN_D], heads)
            v = _stack(lambda h: cs_ref[tok, 2 * GROUP_WIDTH + h * GDN_D:2 * GROUP_WIDTH + (h + 1) * GDN_D], heads)
            q = q * (lax.rsqrt(jnp.sum(q * q, axis=-1, keepdims=True) + EPS) * GDN_D ** -0.5)
            k = k * lax.rsqrt(jnp.sum(k * k, axis=-1, keepdims=True) + EPS)
            gam_c = _stack(lambda h: gam_all[:, GA_OFF + h:GA_OFF + h + 1], heads)
            beta_c = _stack(lambda h: beta_all[:, GB_OFF + h:GB_OFF + h + 1], heads)
            gam_r = _row_form(gam_t[GA_OFF + heads[0]:GA_OFF + heads[0] + GDN_STACK, :], pick)
            decay = jnp.where(causal, jnp.exp(jnp.where(causal, gam_c - gam_r, 0.0)), 0.0)
            a_mats.append(jnp.where(strict, _dot_nt(k, k) * decay * beta_c, 0.0))
            qs.append(q); ks.append(k); vs.append(v)
            gam_cs.append(gam_c); beta_cs.append(beta_c); decays.append(decay)

        invs = _unit_lower_inverses(a_mats, srow, scol)
        out = []
        for g, heads in enumerate(groups):
            q, k, v, gam_c, beta_c = qs[g], ks[g], vs[g], gam_cs[g], beta_cs[g]
            egam = jnp.exp(gam_c)
            rhs = jnp.concatenate([v * beta_c, k * (beta_c * egam)], axis=1)
            sol = _dot(invs[g], rhs)
            qk = _dot_nt(q, k) * decays[g]
            gam_last = _stack_bcast(lambda h: gam_all[CHUNK - 1:CHUNK, GA_OFF + h:GA_OFF + h + 1], heads)
            k_end = k * jnp.exp(gam_last - gam_c)
            c_decs = [jnp.exp(gam_all[CHUNK - 1:CHUNK, GA_OFF + h:GA_OFF + h + 1]) for h in heads]
            out.append((sol, qk, q * egam, k_end, c_decs))
        return out

    def state_step(ci, pre):
        tok = _chunk_rows(ci)
        for g, heads in enumerate(groups):
            sol, qk, q_dec, k_end, c_decs = pre[g]
            v_new, q_state, states = [], [], []
            for j, h in enumerate(heads):
                sl = _head_rows(j)
                state = s_ref[h]
                both = _dot(jnp.concatenate([sol[sl, GDN_D:], q_dec[sl]], axis=0), state)
                v_new.append(sol[sl, :GDN_D] - both[:CHUNK])
                q_state.append(both[CHUNK:])
                states.append(state)
            o = jnp.concatenate(q_state, axis=0) + _dot(qk, jnp.concatenate(v_new, axis=0))
            for j, h in enumerate(heads):
                s_ref[h] = states[j] * c_decs[j] + _dot_tn(k_end[_head_rows(j)], v_new[j])
            o = o * lax.rsqrt(jnp.mean(o * o, axis=-1, keepdims=True) + EPS) * normw_ref[...]
            for j, h in enumerate(heads):
                zz = z_ref[0, tok, h * GDN_D:(h + 1) * GDN_D]
                o_ref[0, tok, h * GDN_D:(h + 1) * GDN_D] = (
                    o[_head_rows(j)] * (zz * _sigmoid(zz))).astype(o_ref.dtype)

    return intra, state_step


def _mlstm_parts(q_ref, k_ref, v_ref, og_ref, gate_ref, gb_ref, normw_ref, o_ref, c_ref, n_ref, m_ref):
    @pl.when(pl.program_id(1) == 0)
    def _():
        c_ref[...] = jnp.zeros_like(c_ref)
        n_ref[...] = jnp.zeros_like(n_ref)
        m_ref[...] = jnp.zeros_like(m_ref)

    groups = _head_groups(ML_HEADS, ML_STACK)
    row, col = _chunk_masks()
    _, _, causal, _, pick = _stack_masks(ML_STACK)

    m_sts = [_stack_bcast(lambda h: m_ref[h, 0:1, 0:1], heads) for heads in groups]

    def weights(ci):
        tok = _chunk_rows(ci)
        pre = gate_ref[0, tok, :] + gb_ref[...]
        fc_all = _cumsum_rows(_log_sigmoid(pre), row, col)
        fc_t = fc_all.T
        pre_t = pre.T
        per_group = []
        for g, heads in enumerate(groups):
            h0 = heads[0]
            q = _stack(lambda h: q_ref[0, tok, h * ML_DK:(h + 1) * ML_DK], heads)
            k = _stack(lambda h: k_ref[0, tok, h * ML_DK:(h + 1) * ML_DK], heads) * ML_DK ** -0.5
            v = _stack(lambda h: v_ref[0, tok, h * ML_DV:(h + 1) * ML_DV], heads)
            fc_c = _stack(lambda h: fc_all[:, MF_OFF + h:MF_OFF + h + 1], heads)
            ic_c = _stack(lambda h: pre[:, MI_OFF + h:MI_OFF + h + 1], heads)
            fc_last = _stack_bcast(lambda h: fc_all[CHUNK - 1:CHUNK, MF_OFF + h:MF_OFF + h + 1], heads)
            fc_r = _row_form(fc_t[MF_OFF + h0:MF_OFF + h0 + ML_STACK, :], pick)
            ic_r = _row_form(pre_t[MI_OFF + h0:MI_OFF + h0 + ML_STACK, :], pick)
            d_log = jnp.where(causal, fc_c - fc_r + ic_r, -jnp.inf)
            m_intra = jnp.max(d_log, axis=-1, keepdims=True)
            qk = _dot_nt(q, k)
            src_end = fc_last - fc_c + ic_c
            m_src = jnp.concatenate(
                [jnp.broadcast_to(jnp.max(src_end[_head_rows(j)], axis=0, keepdims=True), (CHUNK, 1))
                 for j in range(ML_STACK)], axis=0)
            m_st = m_sts[g]
            m_inter = fc_c + m_st
            m_t = jnp.maximum(m_inter, m_intra)
            w_inter = jnp.exp(m_inter - m_t)
            w_intra = jnp.exp(d_log - m_t) * qk
            m_new = jnp.maximum(fc_last + m_st, m_src)
            w_state = jnp.exp(fc_last + m_st - m_new)
            kw = k * jnp.exp(src_end - m_new)
            m_sts[g] = m_new
            per_group.append((q, w_inter, _dot(w_intra, v), jnp.sum(w_intra, axis=-1, keepdims=True),
                              jnp.exp(-m_t), w_state,
                              [_dot_tn(kw[_head_rows(j)], v[_head_rows(j)]) for j in range(ML_STACK)],
                              [jnp.sum(kw[_head_rows(j)], axis=0, keepdims=True) for j in range(ML_STACK)]))
        return per_group

    c_sts = [[c_ref[h] for h in heads] for heads in groups]
    n_sts = [[n_ref[h, 0:1, :] for h in heads] for heads in groups]

    def recur(ci, cur):
        tok = _chunk_rows(ci)
        for g, heads in enumerate(groups):
            q, w_inter, intra_num, intra_den, floor, w_state, kv, ksum = cur[g]
            q_c = jnp.concatenate([_dot(q[_head_rows(j)], c_sts[g][j]) for j in range(ML_STACK)], axis=0)
            n_rows = jnp.concatenate(
                [jnp.broadcast_to(n_sts[g][j], (CHUNK, ML_DK)) for j in range(ML_STACK)], axis=0)
            num = w_inter * q_c + intra_num
            den = w_inter * jnp.sum(q * n_rows, axis=-1, keepdims=True) + intra_den
            hh = num / jnp.maximum(jnp.abs(den), floor)
            hh = hh * lax.rsqrt(jnp.mean(hh * hh, axis=-1, keepdims=True) + EPS) * normw_ref[...]
            for j, h in enumerate(heads):
                gg = og_ref[0, tok, h * ML_DV:(h + 1) * ML_DV]
                o_ref[0, tok, h * ML_DV:(h + 1) * ML_DV] = (hh[_head_rows(j)] * _sigmoid(gg)).astype(o_ref.dtype)
                ws = w_state[j * CHUNK:j * CHUNK + 1, :]
                c_sts[g][j] = ws * c_sts[g][j] + kv[j]
                n_sts[g][j] = ws * n_sts[g][j] + ksum[j]

    def finish():
        for g, heads in enumerate(groups):
            for j, h in enumerate(heads):
                c_ref[h] = c_sts[g][j]
                n_ref[h] = jnp.broadcast_to(n_sts[g][j], (8, ML_DK))
                m_ref[h] = jnp.broadcast_to(m_sts[g][j * CHUNK:j * CHUNK + 1, :], (8, LANES))

    return weights, recur, finish


def _pipelined_chunks(n_chunks, prepare, consume):
    cur = prepare(0)
    for ci in range(n_chunks):
        nxt = prepare(ci + 1) if ci + 1 < n_chunks else None
        consume(ci, cur)
        cur = nxt


def _gdn_kernel(*refs, tb):
    intra, state_step = _gdn_parts(*refs, tb=tb)
    _pipelined_chunks(tb // CHUNK, intra, state_step)


def _mlstm_kernel(*refs, tb):
    weights, recur, finish = _mlstm_parts(*refs)
    _pipelined_chunks(tb // CHUNK, weights, recur)
    finish()


def _gdn_mixer(y, yg, conv_w, alog_row, dtb_row, norm_w):
    b, t, _ = y.shape
    tb = min(GDN_BLOCK, t)
    return _mixer_call(
        functools.partial(_gdn_kernel, tb=tb),
        [_tok_spec(tb, GDN_QKV, 0), _tok_spec(tb, GROUP_WIDTH, GDN_QKV // GROUP_WIDTH), _tok_spec(tb, LANES, 0)],
        [_full_spec((CONV_K, GDN_QKV)), _full_spec((1, LANES)), _full_spec((1, LANES)), _full_spec((1, GDN_D))],
        [pltpu.VMEM((tb + 8, GDN_QKV), F32), pltpu.VMEM((tb, GDN_QKV), F32),
         pltpu.VMEM((GDN_HEADS, GDN_D, GDN_D), F32)],
        (y, y, yg, conv_w, alog_row, dtb_row, norm_w), b, t, tb, "gdn_mixer")


def _mlstm_mixer(y, yg, gb_row, norm_w):
    b, t, _ = y.shape
    tb = min(MIX_BLOCK, t)
    qk_w = ML_HEADS * ML_DK
    base = GDN_QKV + GROUP_WIDTH
    vblk = (base + 2 * qk_w) // GROUP_WIDTH
    return _mixer_call(
        functools.partial(_mlstm_kernel, tb=tb),
        [_tok_spec(tb, qk_w, base // qk_w), _tok_spec(tb, qk_w, base // qk_w + 1),
         _tok_spec(tb, GROUP_WIDTH, vblk), _tok_spec(tb, GROUP_WIDTH, vblk + 1), _tok_spec(tb, LANES, 0)],
        [_full_spec((1, LANES)), _full_spec((1, ML_DV))],
        [pltpu.VMEM((ML_HEADS, ML_DK, ML_DV), F32), pltpu.VMEM((ML_HEADS, 8, ML_DK), F32),
         pltpu.VMEM((ML_HEADS, 8, LANES), F32)],
        (y, y, y, y, yg, gb_row, norm_w), b, t, tb, "mlstm_mixer")


def _gla_intra(q, k, v, bcum, bcum_last, stack, causal):
    q_dec = q * jnp.exp(bcum)
    k_inv = k * jnp.exp(-bcum)
    attn = jnp.where(causal, _dot_nt(q_dec, k_inv), 0.0)
    o_intra = _dot(attn, v)
    k_end = k * jnp.exp(bcum_last - bcum)
    kv = [_dot_tn(k_end[_head_rows(j)], v[_head_rows(j)]) for j in range(stack)]
    return q_dec, o_intra, kv


def _gla_recur(q_dec, o_intra, kv, end_decay_cols, states):
    o = o_intra + jnp.concatenate(
        [_dot(q_dec[_head_rows(j)], states[j]) for j in range(len(states))], axis=0)
    return o, [states[j] * end_decay_cols[j] + kv[j] for j in range(len(states))]


def _ret_kernel(q_ref, k_ref, v_ref, g_ref, cos_ref, sin_ref, o_ref, s_ref, *, tb):
    @pl.when(pl.program_id(1) == 0)
    def _():
        s_ref[...] = jnp.zeros_like(s_ref)

    groups = _head_groups(RET_HEADS, RET_STACK)

    _, _, causal, _, _ = _stack_masks(RET_STACK)
    steps = (lax.broadcasted_iota(jnp.int32, (CHUNK, 1), 0) + 1).astype(F32)

    def intra(ci, heads):
        tok = _chunk_rows(ci)
        cos2 = jnp.concatenate([cos_ref[0, tok, :]] * RET_STACK, axis=0)
        sin2 = jnp.concatenate([sin_ref[0, tok, :]] * RET_STACK, axis=0)
        log_gamma = [math.log(1.0 - 2.0 ** (-5.0 - h)) for h in heads]
        q = _stack(lambda h: q_ref[0, tok, h * RET_DK:(h + 1) * RET_DK], heads)
        k = _stack(lambda h: k_ref[0, tok, h * RET_DK:(h + 1) * RET_DK], heads)
        v = _stack(lambda h: v_ref[0, tok, h * RET_DV:(h + 1) * RET_DV], heads)
        q = q * cos2 + pltpu.roll(q, RET_DK // 2, 1) * sin2
        k = (k * cos2 + pltpu.roll(k, RET_DK // 2, 1) * sin2) * RET_DK ** -0.5
        bcum = jnp.concatenate([steps * lg for lg in log_gamma], axis=0)
        bcum_last = jnp.concatenate([jnp.full((CHUNK, 1), CHUNK * lg, F32) for lg in log_gamma], axis=0)
        return _gla_intra(q, k, v, bcum, bcum_last, RET_STACK, causal)

    n_chunks = tb // CHUNK
    pre = [[intra(ci, heads) for heads in groups] for ci in range(n_chunks)]
    for g, heads in enumerate(groups):
        ends = [jnp.exp(jnp.full((1, 1), CHUNK * math.log(1.0 - 2.0 ** (-5.0 - h)), F32)) for h in heads]
        states = [s_ref[h] for h in heads]
        for ci in range(n_chunks):
            tok = _chunk_rows(ci)
            o, states = _gla_recur(*pre[ci][g], ends, states)
            o = o * lax.rsqrt(jnp.mean(o * o, axis=-1, keepdims=True) + EPS)
            for j, h in enumerate(heads):
                gg = g_ref[0, tok, h * RET_DV:(h + 1) * RET_DV]
                o_ref[0, tok, h * RET_DV:(h + 1) * RET_DV] = (
                    o[_head_rows(j)] * (gg * _sigmoid(gg))).astype(o_ref.dtype)
        for j, h in enumerate(heads):
            s_ref[h] = states[j]


def _ret_mixer(y, cos2, sin2):
    b, t, _ = y.shape
    tb = min(MIX_BLOCK, t)
    qk_w = RET_HEADS * RET_DK
    return _mixer_call(
        functools.partial(_ret_kernel, tb=tb),
        [_tok_spec(tb, qk_w, 0), _tok_spec(tb, qk_w, 1), _tok_spec(tb, GROUP_WIDTH, 1),
         _tok_spec(tb, GROUP_WIDTH, 2), _tok_spec(tb, RET_DK, 0), _tok_spec(tb, RET_DK, 0)],
        [],
        [pltpu.VMEM((RET_HEADS, RET_DK, RET_DV), F32)],
        (y, y, y, y, cos2, sin2), b, t, tb, "ret_mixer")


def _gla_kernel(q_ref, k_ref, v_ref, r_ref, la_ref, wup_ref, bup_ref, normw_ref, o_ref, s_ref, *, tb):
    @pl.when(pl.program_id(1) == 0)
    def _():
        s_ref[...] = jnp.zeros_like(s_ref)

    groups = _head_groups(GLA_HEADS, GLA_STACK)

    row, col = _chunk_masks()
    _, _, causal, _, _ = _stack_masks(GLA_STACK)

    for ci in range(tb // CHUNK):
        tok = _chunk_rows(ci)
        gate = _dot(la_ref[0, tok, :], wup_ref[...]) + bup_ref[...]
        bcum_all = _cumsum_rows(_log_sigmoid(gate) * (1.0 / GLA_TAU), row, col)
        for heads in groups:
            q = _stack(lambda h: q_ref[0, tok, h * GLA_DK:(h + 1) * GLA_DK], heads) * GLA_DK ** -0.5
            k = _stack(lambda h: k_ref[0, tok, h * GLA_DK:(h + 1) * GLA_DK], heads)
            v = _stack(lambda h: v_ref[0, tok, h * GLA_DV:(h + 1) * GLA_DV], heads)
            bcum = _stack(lambda h: bcum_all[:, h * GLA_DK:(h + 1) * GLA_DK], heads)
            bcum_last = _stack_bcast(lambda h: bcum_all[CHUNK - 1:CHUNK, h * GLA_DK:(h + 1) * GLA_DK], heads)
            ends = [jnp.exp(bcum_all[:, h * GLA_DK:(h + 1) * GLA_DK].T[:, CHUNK - 1:CHUNK]) for h in heads]
            q_dec = q * jnp.exp(bcum)
            k_inv = k * jnp.exp(-bcum)
            attn = jnp.where(causal, _dot_nt(q_dec, k_inv), 0.0)
            states = [s_ref[h] for h in heads]
            o = _dot(attn, v) + jnp.concatenate(
                [_dot(q_dec[_head_rows(j)], states[j]) for j in range(GLA_STACK)], axis=0)
            k_end = k * jnp.exp(bcum_last - bcum)
            for j, h in enumerate(heads):
                s_ref[h] = states[j] * ends[j] + _dot_tn(k_end[_head_rows(j)], v[_head_rows(j)])
            o = o * lax.rsqrt(jnp.mean(o * o, axis=-1, keepdims=True) + EPS) * normw_ref[...]
            for j, h in enumerate(heads):
                rr = r_ref[0, tok, h * GLA_DV:(h + 1) * GLA_DV]
                o_ref[0, tok, h * GLA_DV:(h + 1) * GLA_DV] = (
                    o[_head_rows(j)] * (rr * _sigmoid(rr))).astype(o_ref.dtype)


def _gla_mixer(y, yla, w_up_pad, b_up, norm_w):
    b, t, _ = y.shape
    tb = min(GLA_BLOCK, t)
    qk_w = GLA_HEADS * GLA_DK
    base = 2 * RET_HEADS * RET_DK + 2 * GROUP_WIDTH
    vblk = (base + 2 * qk_w) // GROUP_WIDTH
    return _mixer_call(
        functools.partial(_gla_kernel, tb=tb),
        [_tok_spec(tb, qk_w, base // qk_w), _tok_spec(tb, qk_w, base // qk_w + 1),
         _tok_spec(tb, GROUP_WIDTH, vblk), _tok_spec(tb, GROUP_WIDTH, vblk + 1), _tok_spec(tb, LANES, 0)],
        [_full_spec((LANES, qk_w)), _full_spec((1, qk_w)), _full_spec((1, GLA_DV))],
        [pltpu.VMEM((GLA_HEADS, GLA_DK, GLA_DV), F32)],
        (y, y, y, y, yla, w_up_pad, b_up, norm_w), b, t, tb, "gla_mixer")


def _pad_lanes(v, offset=0):
    return jnp.zeros((1, LANES), F32).at[0, offset:offset + v.shape[0]].set(v.astype(F32))


_EVEN_G_END = GDN_QKV + GROUP_WIDTH
_EVEN_M_START = _EVEN_G_END + 2 * GDN_HEADS
_EVEN_M_END = _EVEN_M_START + EVEN_MAIN - _EVEN_G_END


def _in_proj_weight(layer, e_w_in, o_w_in):
    j = layer // 2
    if layer % 2 == 0:
        w_in, main = e_w_in[j], EVEN_MAIN
        parts = [w_in[:, :_EVEN_G_END], w_in[:, _EVEN_M_START:_EVEN_M_END],
                 w_in[:, _EVEN_G_END:_EVEN_M_START], w_in[:, _EVEN_M_END:]]
    else:
        w_in, main = o_w_in[j], ODD_MAIN
        parts = [w_in]
    used = sum(p.shape[1] for p in parts)
    parts.append(jnp.zeros((w_in.shape[0], main + LANES - used), w_in.dtype))
    return jnp.concatenate(parts, axis=1).astype(BF16), main


def _even_mixer(xb, yg2, w_in, conv_w, a_log, dt_bias, gdn_norm_w, ml_gate_b, ml_norm_w, w_out_all, j):
    b, t, d = xb.shape
    x2 = xb.reshape(b * t, d)
    w_out = _cast_bf16(w_out_all, j, "even_w_out_cast")
    y = _matmul(x2, w_in, n=EVEN_MAIN, out_dtype=F32, tm=1024, tn=1024, tk=d,
                name="even_in_proj").reshape(b, t, EVEN_MAIN)
    yg = yg2.reshape(b, t, LANES)
    o_a = _gdn_mixer(y, yg, conv_w.astype(F32), _pad_lanes(a_log, GA_OFF), _pad_lanes(dt_bias, GA_OFF),
                     gdn_norm_w.reshape(1, GDN_D).astype(F32))
    o_b = _mlstm_mixer(y, yg, _pad_lanes(ml_gate_b, MI_OFF), ml_norm_w.reshape(1, ML_DV).astype(F32))
    return _matmul_pair(o_a.reshape(b * t, GROUP_WIDTH), o_b.reshape(b * t, GROUP_WIDTH), w_out,
                        out_dtype=BRANCH_DTYPE, tm=1024, tn=1024, name="even_out_proj")


def _odd_mixer(xb, yg2, positions, w_in, gla_w_up, gla_b_up, gla_norm_w, w_out_all, j):
    b, t, d = xb.shape
    x2 = xb.reshape(b * t, d)
    w_out = _cast_bf16(w_out_all, j, "odd_w_out_cast")
    y = _matmul(x2, w_in, n=ODD_MAIN, out_dtype=F32, tm=1024, tn=1024, tk=d,
                name="odd_in_proj").reshape(b, t, ODD_MAIN)
    yla = yg2.reshape(b, t, LANES)
    inv_freq = 1.0 / (ROPE_BASE ** jnp.linspace(0.0, 1.0, RET_DK // 2, dtype=F32))
    theta = positions.astype(F32)[:, :, None] * inv_freq
    cos, sin = jnp.cos(theta), jnp.sin(theta)
    cos2 = jnp.concatenate([cos, cos], axis=-1)
    sin2 = jnp.concatenate([-sin, sin], axis=-1)
    o_c = _ret_mixer(y, cos2, sin2)
    w_up_pad = jnp.pad(gla_w_up.astype(F32), ((0, LANES - GLA_RANK), (0, 0)))
    o_d = _gla_mixer(y, yla, w_up_pad, gla_b_up.reshape(1, -1).astype(F32),
                     gla_norm_w.reshape(1, GLA_DV).astype(F32))
    return _matmul_pair(o_c.reshape(b * t, GROUP_WIDTH), o_d.reshape(b * t, GROUP_WIDTH), w_out,
                        out_dtype=BRANCH_DTYPE, tm=1024, tn=1024, name="odd_out_proj")


def _mlp(xb2, w_up_all, w_down_all, layer):
    w_up = _cast_bf16(w_up_all, layer, f"mlp_w_up_cast_{layer}")
    w_down = _cast_bf16(w_down_all, layer, f"mlp_w_down_cast_{layer}")
    hdn = _matmul(xb2, w_up, out_dtype=BF16, tm=1024, tn=1024, tk=D_MODEL, relu2=True, name=f"mlp_up_{layer}")
    return _matmul(hdn, w_down, out_dtype=BRANCH_DTYPE, tm=1024, tn=1024, tk=4096, name=f"mlp_down_{layer}")


def kernel(x, positions, e_w_in, e_conv_w, e_a_log, e_dt_bias, e_gdn_norm_w, e_mlstm_gate_b, e_mlstm_norm_w, e_w_out, o_w_in, o_gla_w_up, o_gla_b_up, o_gla_norm_w, o_w_out, ln_mix_g, ln_mix_b, mlp_w_up, mlp_w_down, ln_mlp_g, ln_mlp_b):
    b, t, d = x.shape
    x2 = x.reshape(b * t, d)
    w_in, main = _in_proj_weight(0, e_w_in, o_w_in)
    xb2, yg2 = _cast_and_gate(x2, (w_in, main // LANES), name="input_cast_gate")
    for layer in range(DEPTH):
        j = layer // 2
        xb = xb2.reshape(b, t, d)
        if layer % 2 == 0:
            h = _even_mixer(xb, yg2, w_in, e_conv_w[j], e_a_log[j], e_dt_bias[j], e_gdn_norm_w[j],
                            e_mlstm_gate_b[j], e_mlstm_norm_w[j], e_w_out, j)
        else:
            h = _odd_mixer(xb, yg2, positions, w_in, o_gla_w_up[j], o_gla_b_up[j], o_gla_norm_w[j], o_w_out, j)
        x2, xb2, _ = _ln_residual(x2, h, ln_mix_g[layer], ln_mix_b[layer], name=f"ln_mix_{layer}")
        f = _mlp(xb2, mlp_w_up, mlp_w_down, layer)
        more = layer + 1 < DEPTH
        if more:
            w_in, main = _in_proj_weight(layer + 1, e_w_in, o_w_in)
        x2, xb2, yg2 = _ln_residual(x2, f, ln_mlp_g[layer], ln_mlp_b[layer], with_bf16=more,
                                    next_gate_w=(w_in, main // LANES) if more else None,
                                    name=f"ln_mlp_{layer}")
    return x2.reshape(b, t, d)
```
